```python
import math
import jax, jax.numpy as jnp
from jax import lax
import numpy as np

D_MODEL = 4096
BATCH = 16
SEQ = 256
DEPTH = 4
DEC_BATCH = 2
DEC_SEQ = 1024
PAST_LEN = 512

GRID_W = 64
DA_HEADS = 8
DA_QK_DIM = 128
DA_V_DIM = 2 * DA_QK_DIM
GLA_HEADS = 8
GLA_K_DIM = 128
GLA_V_DIM = 256
GLA_LOWRANK = 16
GLA_TAU = 16.0
GLA_CHUNK = 64
N_GROUPS = 4
EXPERTS_PER_GROUP = 8
N_EXPERTS = N_GROUPS * EXPERTS_PER_GROUP
TOP_K = 2
EXPERT_FF = 512
MOE_BLOCK = 128
Q_BLOCK = 128
ROPE_THETA = 10000.0
ROPE_FREQS = DA_QK_DIM // 4
LN_EPS = 1e-5
DEEPNORM_ALPHA = (2.0 * DEPTH) ** 0.25
DEEPNORM_BETA = (8.0 * DEPTH) ** -0.25

DA_Q_W = DA_HEADS * 2 * DA_QK_DIM
DA_V_W = DA_HEADS * DA_V_DIM
GLA_Q_W = GLA_HEADS * GLA_K_DIM
GLA_V_W = GLA_HEADS * GLA_V_DIM
IN_SPLITS = (DA_Q_W, DA_Q_W, DA_V_W, GLA_Q_W, GLA_Q_W, GLA_V_W, GLA_V_W, 2 * GLA_LOWRANK, D_MODEL, D_MODEL)
N_IN = 2 * DA_Q_W + DA_V_W + 2 * GLA_Q_W + 2 * GLA_V_W + 2 * GLA_LOWRANK + 2 * D_MODEL

kernel_name = 'hybrid_diffattn_gla_hmoe_diffusion_step'


def _layer_norm(x, g, b):
    xf = x.astype(jnp.float32)
    mu = jnp.mean(xf, axis=-1, keepdims=True)
    xc = xf - mu
    var = jnp.mean(xc * xc, axis=-1, keepdims=True)
    y = xc * lax.rsqrt(var + LN_EPS) * g.astype(jnp.float32) + b.astype(jnp.float32)
    return y.astype(x.dtype)


def _rms_norm(x, w):
    xf = x.astype(jnp.float32)
    y = xf * lax.rsqrt(jnp.mean(xf * xf, axis=-1, keepdims=True) + LN_EPS)
    return (y * w.astype(jnp.float32)).astype(x.dtype)


def _rope_tables(n_tokens):
    rows = n_tokens // GRID_W
    row = jnp.repeat(jnp.arange(rows, dtype=jnp.float32), GRID_W)
    col = jnp.tile(jnp.arange(GRID_W, dtype=jnp.float32), rows)
    inv = 1.0 / (ROPE_THETA ** (jnp.arange(ROPE_FREQS, dtype=jnp.float32) / ROPE_FREQS))
    ang = jnp.stack([row[:, None] * inv, col[:, None] * inv], axis=1)
    return jnp.cos(ang), jnp.sin(ang)


def _apply_rope(x, cos, sin):
    shp = x.shape
    xr = x.astype(jnp.float32).reshape(shp[:-1] + (2, 2, ROPE_FREQS))
    x1 = xr[..., 0, :]
    x2 = xr[..., 1, :]
    c = cos[None, :, None, None]
    s = sin[None, :, None, None]
    out = jnp.stack([x1 * c - x2 * s, x2 * c + x1 * s], axis=-2)
    return out.reshape(shp).astype(x.dtype)


def _diff_attention(q, k, v, lam):
    B, Lq = q.shape[0], q.shape[1]
    nb = Lq // Q_BLOCK
    qb = jnp.moveaxis(q.reshape((B, nb, Q_BLOCK) + q.shape[2:]), 1, 0)
    scale = DA_QK_DIM ** -0.5

    def one_block(qi):
        s = jnp.einsum('bqhcd,bkhcd->bchqk', qi, k, preferred_element_type=jnp.float32) * scale
        p = jax.nn.softmax(s, axis=-1)
        a = p[:, 0] - lam * p[:, 1]
        return jnp.einsum('bhqk,bkhd->bqhd', a.astype(v.dtype), v)

    o = lax.map(one_block, qb)
    return jnp.moveaxis(o, 0, 1).reshape((B, Lq) + v.shape[2:])


def _gla_chunked(q, k, v, log_a, s0):
    B, L, H, K = q.shape
    V = v.shape[-1]
    C = GLA_CHUNK
    N = L // C
    f32 = jnp.float32
    qc = q.reshape(B, N, C, H, K).astype(f32) * (K ** -0.5)
    kc = k.reshape(B, N, C, H, K).astype(f32)
    vc = v.reshape(B, N, C, H, V).astype(f32)
    bcum = jnp.cumsum(log_a.reshape(B, N, C, H, K).astype(f32), axis=2)
    blast = bcum[:, :, -1]
    q_in = qc * jnp.exp(bcum)
    k_in = kc * jnp.exp(-bcum)
    k_out = kc * jnp.exp(blast[:, :, None] - bcum)
    mask = jnp.tril(jnp.ones((C, C), dtype=bool))
    att = jnp.einsum('bnthk,bnshk->bnhts', q_in, k_in)
    att = jnp.where(mask, att, 0.0)
    o_intra = jnp.einsum('bnhts,bnshv->bnthv', att, vc)
    ds = jnp.einsum('bnshk,bnshv->bnhkv', k_out, vc)

    def step(s, xs):
        q_n, dec_n, ds_n = xs
        o_n = jnp.einsum('bthk,bhkv->bthv', q_n, s)
        return dec_n[..., None] * s + ds_n, o_n

    xs = (jnp.moveaxis(q_in, 1, 0), jnp.moveaxis(jnp.exp(blast), 1, 0), jnp.moveaxis(ds, 1, 0))
    s_fin, o_inter = lax.scan(step, s0.astype(f32), xs)
    o = o_intra + jnp.moveaxis(o_inter, 0, 1)
    return o.reshape(B, L, H, V).astype(v.dtype), s_fin


def _bi_gla(q, k, v, la_f, la_b, s0_f, s0_b):
    o_f, s_f = _gla_chunked(q, k, v, la_f, s0_f)
    fl = lambda t: jnp.flip(t, axis=1)
    o_b, s_b = _gla_chunked(fl(q), fl(k), fl(v), fl(la_b), s0_b)
    return o_f + fl(o_b), s_f, s_b


def _token_mixer(h, lp, l, rope, ctx):
    B, L, _ = h.shape
    u = h @ lp['w_in']
    offs = np.cumsum(IN_SPLITS)[:-1].tolist()
    qd, kd, vd, qg, kg, vg, rg, ag, gd, gg = jnp.split(u, offs, axis=-1)
    qd = qd.reshape(B, L, DA_HEADS, 2, DA_QK_DIM)
    kd = kd.reshape(B, L, DA_HEADS, 2, DA_QK_DIM)
    vd = vd.reshape(B, L, DA_HEADS, DA_V_DIM)
    if rope is not None:
        qd = _apply_rope(qd, rope[0], rope[1])
        kd = _apply_rope(kd, rope[0], rope[1])
    lam_init = 0.8 - 0.6 * math.exp(-0.3 * l)
    lv = lp['da_lambda'].astype(jnp.float32)
    lam = jnp.exp(jnp.sum(lv[0] * lv[1])) - jnp.exp(jnp.sum(lv[2] * lv[3])) + lam_init
    if ctx is None:
        k_all, v_all = kd, vd
        s0_f = jnp.zeros((B, GLA_HEADS, GLA_K_DIM, GLA_V_DIM), jnp.float32)
        s0_b = s0_f
    else:
        ck, cv, s0_f, s0_b = ctx
        k_all = jnp.concatenate([ck.astype(kd.dtype), kd], axis=1)
        v_all = jnp.concatenate([cv.astype(vd.dtype), vd], axis=1)
    od = _diff_attention(qd, k_all, v_all, lam)
    od = _rms_norm(od, lp['da_norm_w']) * (1.0 - lam_init)
    a_logit = jnp.einsum('blcr,crk->blck', ag.reshape(B, L, 2, GLA_LOWRANK), lp['gla_a2']) + lp['gla_a_bias']
    log_a = (jax.nn.log_sigmoid(a_logit.astype(jnp.float32)) / GLA_TAU).reshape(B, L, 2, GLA_HEADS, GLA_K_DIM)
    og, s_f, s_b = _bi_gla(qg.reshape(B, L, GLA_HEADS, GLA_K_DIM), kg.reshape(B, L, GLA_HEADS, GLA_K_DIM),
                           vg.reshape(B, L, GLA_HEADS, GLA_V_DIM), log_a[:, :, 0], log_a[:, :, 1], s0_f, s0_b)
    og = _rms_norm(og, lp['gla_norm_w']) * jax.nn.silu(rg.reshape(B, L, GLA_HEADS, GLA_V_DIM))
    yd = od.reshape(B, L, DA_V_W) @ lp['w_da_proj']
    yg = og.reshape(B, L, GLA_V_W) @ lp['w_gla_proj']
    m = jax.nn.sigmoid(gd + lp['b_gate'][0]) * yd + jax.nn.sigmoid(gg + lp['b_gate'][1]) * yg
    return m @ lp['w_out'], kd, vd, s_f, s_b


def _hier_moe(h, lp):
    B, L, D = h.shape
    T = B * L
    x = h.reshape(T, D)
    pg = jax.nn.softmax((x @ lp['router_g_w'] + lp['router_g_b']).astype(jnp.float32), axis=-1)
    g_idx = jnp.argmax(pg, axis=-1)
    p_sel = jnp.take_along_axis(pg, g_idx[:, None], axis=-1)
    le = jnp.einsum('td,dge->tge', x, lp['router_e_w']) + lp['router_e_b']
    le = jnp.take_along_axis(le, g_idx[:, None, None], axis=1)[:, 0]
    pe = jax.nn.softmax(le.astype(jnp.float32), axis=-1)
    top_p, top_i = lax.top_k(pe, TOP_K)
    gate = p_sel * top_p / jnp.sum(top_p, axis=-1, keepdims=True)
    expert = g_idx[:, None] * EXPERTS_PER_GROUP + top_i
    A = T * TOP_K
    e_flat = expert.reshape(A)
    w_flat = gate.reshape(A)
    tok_flat = jnp.repeat(jnp.arange(T, dtype=jnp.int32), TOP_K)
    order = jnp.argsort(e_flat)
    e_sorted = e_flat[order]
    counts = jnp.bincount(e_flat, length=N_EXPERTS)
    starts = jnp.cumsum(counts) - counts
    padded = (counts + MOE_BLOCK - 1) // MOE_BLOCK * MOE_BLOCK
    pends = jnp.cumsum(padded)
    pstarts = pends - padded
    pos = pstarts[e_sorted] + (jnp.arange(A, dtype=jnp.int32) - starts[e_sorted])
    n_blocks = -(-A // MOE_BLOCK) + N_EXPERTS
    P = n_blocks * MOE_BLOCK
    buf_tok = jnp.zeros((P,), jnp.int32).at[pos].set(tok_flat[order])
    buf_w = jnp.zeros((P,), jnp.float32).at[pos].set(w_flat[order])
    blk_e = jnp.clip(jnp.searchsorted(pends, jnp.arange(n_blocks, dtype=pends.dtype) * MOE_BLOCK, side='right'),
                     0, N_EXPERTS - 1)
    w1, w3, w2 = lp['exp_w1'], lp['exp_w3'], lp['exp_w2']

    def run_block(args):
        toks, e = args
        xb = x[toks]
        hid = jax.nn.silu(xb @ w1[e]) * (xb @ w3[e])
        return hid @ w2[e]

    out = lax.map(run_block, (buf_tok.reshape(n_blocks, MOE_BLOCK), blk_e))
    y = jnp.zeros_like(x).at[buf_tok].add(out.reshape(P, D) * buf_w[:, None].astype(x.dtype))
    return y.reshape(B, L, D)


def _trunk_layer(x, cond, lp, l, rope, ctx):
    mod = jax.nn.silu(cond) @ lp['w_mod'] + lp['b_mod']
    sh1, sc1, g1, sh2, sc2, g2 = [m[:, None, :] for m in jnp.split(mod, 6, axis=-1)]
    h = x * (1.0 + sc1) + sh1
    y, kd, vd, s_f, s_b = _token_mixer(h, lp, l, rope, ctx)
    x = _layer_norm(DEEPNORM_ALPHA * x + g1 * y, lp['ln1_g'], lp['ln1_b'])
    h = x * (1.0 + sc2) + sh2
    y = _hier_moe(h, lp)
    x = _layer_norm(DEEPNORM_ALPHA * x + g2 * y, lp['ln2_g'], lp['ln2_b'])
    return x, kd, vd, s_f, s_b


def setup_inputs(seed: int = 0) -> dict:
    key = jax.random.key(seed)
    ks = iter(jax.random.split(key, 40))
    d = D_MODEL

    def nrm(shape, s):
        return jax.random.normal(next(ks), shape, jnp.float32) * s

    return {
        'x_prompt': nrm((BATCH, SEQ, d), 1.0),
        'x_sample': nrm((DEC_BATCH, DEC_SEQ, d), 1.0),
        'cache_k': nrm((DEC_BATCH, DEPTH, PAST_LEN, DA_HEADS, 2, DA_QK_DIM), 1.0),
        'cache_v': nrm((DEC_BATCH, DEPTH, PAST_LEN, DA_HEADS, DA_V_DIM), 1.0),
        'state_gla': nrm((DEC_BATCH, DEPTH, 2, GLA_HEADS, GLA_K_DIM, GLA_V_DIM), 1.0),
        'c': nrm((DEC_BATCH, d), 1.0),
        'c_ctx': nrm((d,), 1.0),
        'w_mod': nrm((DEPTH, d, 6 * d), 0.5 * d ** -0.5),
        'b_mod': nrm((DEPTH, 6 * d), 0.02),
        'w_in': nrm((DEPTH, d, N_IN), d ** -0.5),
        'b_gate': nrm((DEPTH, 2, d), 0.02),
        'da_lambda': nrm((DEPTH, 4, DA_QK_DIM), 0.1),
        'da_norm_w': 1.0 + nrm((DEPTH, DA_V_DIM), 0.02),
        'gla_a2': nrm((DEPTH, 2, GLA_LOWRANK, GLA_Q_W), GLA_LOWRANK ** -0.5),
        'gla_a_bias': nrm((DEPTH, 2, GLA_Q_W), 0.1),
        'gla_norm_w': 1.0 + nrm((DEPTH, GLA_V_DIM), 0.02),
        'w_da_proj': nrm((DEPTH, DA_V_W, d), DA_V_W ** -0.5),
        'w_gla_proj': nrm((DEPTH, GLA_V_W, d), GLA_V_W ** -0.5),
        'w_out': nrm((DEPTH, d, d), DEEPNORM_BETA * d ** -0.5),
        'ln1_g': 1.0 + nrm((DEPTH, d), 0.02),
        'ln1_b': nrm((DEPTH, d), 0.02),
        'ln2_g': 1.0 + nrm((DEPTH, d), 0.02),
        'ln2_b': nrm((DEPTH, d), 0.02),
        'router_g_w': nrm((DEPTH, d, N_GROUPS), d ** -0.5),
        'router_g_b': nrm((DEPTH, N_GROUPS), 0.01),
        'router_e_w': nrm((DEPTH, d, N_GROUPS, EXPERTS_PER_GROUP), d ** -0.5),
        'router_e_b': nrm((DEPTH, N_GROUPS, EXPERTS_PER_GROUP), 0.01),
        'exp_w1': nrm((DEPTH, N_EXPERTS, d, EXPERT_FF), d ** -0.5),
        'exp_w3': nrm((DEPTH, N_EXPERTS, d, EXPERT_FF), d ** -0.5),
        'exp_w2': nrm((DEPTH, N_EXPERTS, EXPERT_FF, d), DEEPNORM_BETA * EXPERT_FF ** -0.5),
    }


def reference(x_prompt, x_sample, cache_k, cache_v, state_gla, c, c_ctx, w_mod, b_mod, w_in, b_gate,
              da_lambda, da_norm_w, gla_a2, gla_a_bias, gla_norm_w, w_da_proj, w_gla_proj, w_out,
              ln1_g, ln1_b, ln2_g, ln2_b, router_g_w, router_g_b, router_e_w, router_e_b,
              exp_w1, exp_w3, exp_w2):
    layers = [dict(w_mod=w_mod[l], b_mod=b_mod[l], w_in=w_in[l], b_gate=b_gate[l], da_lambda=da_lambda[l],
                   da_norm_w=da_norm_w[l], gla_a2=gla_a2[l], gla_a_bias=gla_a_bias[l], gla_norm_w=gla_norm_w[l],
                   w_da_proj=w_da_proj[l], w_gla_proj=w_gla_proj[l], w_out=w_out[l],
                   ln1_g=ln1_g[l], ln1_b=ln1_b[l], ln2_g=ln2_g[l], ln2_b=ln2_b[l],
                   router_g_w=router_g_w[l], router_g_b=router_g_b[l],
                   router_e_w=router_e_w[l], router_e_b=router_e_b[l],
                   exp_w1=exp_w1[l], exp_w3=exp_w3[l], exp_w2=exp_w2[l]) for l in range(DEPTH)]
    xp = x_prompt
    cond_ctx = c_ctx[None, :]
    ks_list, vs_list, st_list = [], [], []
    for l in range(DEPTH):
        xp, kd, vd, s_f, s_b = _trunk_layer(xp, cond_ctx, layers[l], l, None, None)
        ks_list.append(kd)
        vs_list.append(vd)
        st_list.append(jnp.stack([s_f, s_b], axis=1))
    new_cache_k = jnp.stack(ks_list, axis=1)
    new_cache_v = jnp.stack(vs_list, axis=1)
    new_state_gla = jnp.stack(st_list, axis=1)
    rope = _rope_tables(x_sample.shape[1])
    xs = x_sample
    for l in range(DEPTH):
        ctx = (cache_k[:, l], cache_v[:, l], state_gla[:, l, 0], state_gla[:, l, 1])
        xs, _, _, _, _ = _trunk_layer(xs, c, layers[l], l, rope, ctx)
    return (xp, xs, new_cache_k, new_cache_v, new_state_gla)
```

```python
import functools
import math

import jax
import jax.numpy as jnp
import numpy as np
from jax import lax
from jax.experimental import pallas as pl
from jax.experimental.pallas import tpu as pltpu

F32 = jnp.float32
BF16 = jnp.bfloat16

GRID_W = 64
ROPE_THETA = 10000.0
GLA_TAU = 16.0
GLA_CHUNK = 64
TOP_K = 2
LN_EPS = 1e-5
LANE = 128
VMEM_LIMIT = 56 * 1024 * 1024


def _cparams(n_axes):
    return pltpu.CompilerParams(dimension_semantics=("arbitrary",) * n_axes, vmem_limit_bytes=VMEM_LIMIT)


def _tile(n, pref):
    t = min(n, pref)
    while n % t:
        t //= 2
    return t


def _silu(x):
    return x * (1.0 / (1.0 + jnp.exp(-x)))


def _sigmoid(x):
    return 1.0 / (1.0 + jnp.exp(-x))


def _mm_kernel(x_ref, w_ref, o_ref, wb_ref):
    @pl.when(pl.program_id(1) == 0)
    def _():
        wb_ref[...] = w_ref[...].astype(BF16)

    o_ref[...] = jnp.dot(x_ref[...], wb_ref[...], preferred_element_type=F32).astype(o_ref.dtype)


def _matmul(x, w, l, col0, ncols, *, tm=512, tn=512, out_dtype=F32):
    M, K = x.shape
    tm = _tile(M, tm)
    tn = _tile(ncols, tn)
    assert col0 % tn == 0
    cb0 = col0 // tn
    return pl.pallas_call(
        _mm_kernel,
        grid=(ncols // tn, M // tm),
        in_specs=[
            pl.BlockSpec((tm, K), lambda j, i: (i, 0)),
            pl.BlockSpec((None, K, tn), lambda j, i: (l, 0, cb0 + j)),
        ],
        out_specs=pl.BlockSpec((tm, tn), lambda j, i: (i, j)),
        out_shape=jax.ShapeDtypeStruct((M, ncols), out_dtype),
        scratch_shapes=[pltpu.VMEM((K, tn), BF16)],
        compiler_params=_cparams(2),
    )(x, w)


def _mod_kernel(c_ref, w_ref, b_ref, o_ref):
    a = _silu(c_ref[...]).astype(BF16)
    o_ref[...] = jnp.dot(a, w_ref[...].astype(BF16), preferred_element_type=F32) + b_ref[...]


def _modulation(cond, w_mod, b_mod, *, tn=512):
    depth, D, N = w_mod.shape
    R = cond.shape[0]
    tn = _tile(N, tn)
    return pl.pallas_call(
        _mod_kernel,
        grid=(depth, N // tn),
        in_specs=[
            pl.BlockSpec((R, D), lambda l, j: (0, 0)),
            pl.BlockSpec((None, D, tn), lambda l, j: (l, 0, j)),
            pl.BlockSpec((None, 1, tn), lambda l, j: (l, 0, j)),
        ],
        out_specs=pl.BlockSpec((None, R, tn), lambda l, j: (l, 0, j)),
        out_shape=jax.ShapeDtypeStruct((depth, R, N), F32),
        compiler_params=_cparams(2),
    )(cond, w_mod, b_mod.reshape(depth, 1, N))


class _Slab:
    def __init__(self, t_ctx, dec_seq, n_dec):
        self.t_ctx, self.dec_seq, self.n_dec = t_ctx, dec_seq, n_dec
        self.T = t_ctx + dec_seq * n_dec

    def tile(self, pref):
        t = min(pref, self.t_ctx, self.dec_seq)
        while self.t_ctx % t or self.dec_seq % t:
            t //= 2
        return t

    def mod_spec(self, tm, l, which, D):
        n_ctx = self.t_ctx // tm
        per = self.dec_seq // tm

        def imap(i, *_):
            return (l, jnp.where(i < n_ctx, 0, (i - n_ctx) // per + 1), which, 0, 0)

        return pl.BlockSpec((None, None, None, 1, D), imap)


def _premod_kernel(x_ref, sc_ref, sh_ref, h_ref):
    h_ref[...] = (x_ref[...] * (1.0 + sc_ref[...]) + sh_ref[...]).astype(h_ref.dtype)


def _premod(x, mod5, slab, l):
    T, D = x.shape
    tm = slab.tile(256)
    return pl.pallas_call(
        _premod_kernel,
        grid=(T // tm,),
        in_specs=[pl.BlockSpec((tm, D), lambda i: (i, 0)), slab.mod_spec(tm, l, 1, D), slab.mod_spec(tm, l, 0, D)],
        out_specs=pl.BlockSpec((tm, D), lambda i: (i, 0)),
        out_shape=jax.ShapeDtypeStruct((T, D), BF16),
        compiler_params=_cparams(1),
    )(x, mod5, mod5)


def _softmax_rows(s):
    m = jnp.max(s, axis=-1, keepdims=True)
    e = jnp.exp(s - m)
    return e / jnp.sum(e, axis=-1, keepdims=True)


def _diff_attn_rows(q, k1, k2, v, lam, scale, dk):
    nt = (((1,), (1,)), ((), ()))
    s1 = lax.dot_general(q[:, :dk].astype(BF16), k1, nt, preferred_element_type=F32) * scale
    s2 = lax.dot_general(q[:, dk:].astype(BF16), k2, nt, preferred_element_type=F32) * scale
    a = _softmax_rows(s1) - lam * _softmax_rows(s2)
    return jnp.dot(a.astype(BF16), v, preferred_element_type=F32)


def _rms_rows(o, w):
    return o * lax.rsqrt(jnp.mean(o * o, axis=-1, keepdims=True) + LN_EPS) * w


def _attn_ctx_kernel(lam_ref, q_ref, k_ref, v_ref, nw_ref, o_ref, *, heads, dk, dv, post):
    lam = lam_ref[0, 0]
    scale = dk ** -0.5
    for h in range(heads):
        q = q_ref[:, h * 2 * dk:(h + 1) * 2 * dk]
        k = k_ref[:, h * 2 * dk:(h + 1) * 2 * dk].astype(BF16)
        v = v_ref[:, h * dv:(h + 1) * dv].astype(BF16)
        o = _diff_attn_rows(q, k[:, :dk], k[:, dk:], v, lam, scale, dk)
        o_ref[:, h * dv:(h + 1) * dv] = (_rms_rows(o, nw_ref[...]) * post).astype(o_ref.dtype)


def _attn_ctx(u, lam, norm_w, *, n_seq, L, heads, dk, dv, post):
    qw = heads * 2 * dk
    vw = heads * dv
    assert qw == vw
    kern = functools.partial(_attn_ctx_kernel, heads=heads, dk=dk, dv=dv, post=post)
    return pl.pallas_call(
        kern,
        grid=(n_seq,),
        in_specs=[
            pl.BlockSpec(memory_space=pltpu.SMEM),
            pl.BlockSpec((L, qw), lambda b: (b, 0)),
            pl.BlockSpec((L, qw), lambda b: (b, 1)),
            pl.BlockSpec((L, vw), lambda b: (b, 2)),
            pl.BlockSpec((1, dv), lambda b: (0, 0)),
        ],
        out_specs=pl.BlockSpec((L, vw), lambda b: (b, 0)),
        out_shape=jax.ShapeDtypeStruct((n_seq * L, vw), BF16),
        compiler_params=_cparams(1),
    )(lam, u, u, u, norm_w)


def _rope_rows(x, cos, s_lo, s_hi, dk):
    q4 = dk // 4
    return x * cos + pltpu.roll(x, dk - q4, 1) * s_lo + pltpu.roll(x, q4, 1) * s_hi


def _attn_dec_kernel(lam_ref, q_ref, k_ref, v_ref, ck_ref, cv_ref, cos_ref, slo_ref, shi_ref, nw_ref, o_ref,
                     kb_ref, vb_ref, *, dk, dv, past, L, tq, post):
    lam = lam_ref[0, 0]
    scale = dk ** -0.5
    cos, slo, shi = cos_ref[...], slo_ref[...], shi_ref[...]
    kb_ref[0:past, :] = ck_ref[...].astype(BF16)
    vb_ref[0:past, :] = cv_ref[...].astype(BF16)
    for c in range(2):
        kb_ref[past:past + L, c * dk:(c + 1) * dk] = _rope_rows(
            k_ref[:, c * dk:(c + 1) * dk], cos, slo, shi, dk).astype(BF16)
    vb_ref[past:past + L, :] = v_ref[...].astype(BF16)
    k1 = kb_ref[:, :dk]
    k2 = kb_ref[:, dk:]
    v = vb_ref[...]
    for i in range(L // tq):
        rows = slice(i * tq, (i + 1) * tq)
        q = jnp.concatenate(
            [_rope_rows(q_ref[rows, c * dk:(c + 1) * dk], cos[rows], slo[rows], shi[rows], dk) for c in range(2)],
            axis=1)
        o = _diff_attn_rows(q, k1, k2, v, lam, scale, dk)
        o_ref[rows, :] = (_rms_rows(o, nw_ref[...]) * post).astype(o_ref.dtype)


def _attn_dec(u, row0, cache_k, cache_v, l, rope, lam, norm_w, *, n_seq, L, heads, dk, dv, post):
    past = cache_k.shape[2]
    assert row0 % L == 0
    rb0 = row0 // L
    tq = _tile(L, 256)
    kern = functools.partial(_attn_dec_kernel, dk=dk, dv=dv, past=past, L=L, tq=tq, post=post)
    cos, slo, shi = rope
    tab = pl.BlockSpec((L, dk), lambda b, h: (0, 0))
    return pl.pallas_call(
        kern,
        grid=(n_seq, heads),
        in_specs=[
            pl.BlockSpec(memory_space=pltpu.SMEM),
            pl.BlockSpec((L, 2 * dk), lambda b, h: (rb0 + b, h)),
            pl.BlockSpec((L, 2 * dk), lambda b, h: (rb0 + b, heads + h)),
            pl.BlockSpec((L, dv), lambda b, h: (rb0 + b, 2 * heads + h)),
            pl.BlockSpec((None, None, past, 2 * dk), lambda b, h: (b, l, 0, h)),
            pl.BlockSpec((None, None, past, dv), lambda b, h: (b, l, 0, h)),
            tab, tab, tab,
            pl.BlockSpec((1, dv), lambda b, h: (0, 0)),
        ],
        out_specs=pl.BlockSpec((L, dv), lambda b, h: (b, h)),
        out_shape=jax.ShapeDtypeStruct((n_seq * L, heads * dv), BF16),
        scratch_shapes=[pltpu.VMEM((past + L, 2 * dk), BF16), pltpu.VMEM((past + L, dv), BF16)],
        compiler_params=_cparams(2),
    )(lam, u, u, u, cache_k, cache_v, cos, slo, shi, norm_w)


def _log_sigmoid(x):
    return -(jnp.maximum(-x, 0.0) + jnp.log1p(jnp.exp(-jnp.abs(x))))


def _gla_kernel(*refs, L, K, V, has_s0, want_state):
    q_ref, k_ref, v_ref, r_ref, ag_ref, a2_ref, ab_ref, nw_ref = refs[:8]
    refs = refs[8:]
    s0_ref = None
    if has_s0:
        s0_ref, refs = refs[0], refs[1:]
    o_ref, refs = refs[0], refs[1:]
    sfin_ref = None
    if want_state:
        sfin_ref, refs = refs[0], refs[1:]
    (oacc_ref,) = refs

    C = GLA_CHUNK
    N = L // C
    r2 = a2_ref.shape[1]
    row = lax.broadcasted_iota(jnp.int32, (C, C), 0)
    col = lax.broadcasted_iota(jnp.int32, (C, C), 1)
    nt = (((1,), (1,)), ((), ()))
    tn = (((0,), (0,)), ((), ()))
    hi = lax.Precision.HIGHEST

    for d in range(2):
        keep = (col <= row) if d == 0 else (col >= row)
        tri = keep.astype(F32)
        a2 = a2_ref[d]
        ab = ab_ref[d]
        if has_s0:
            st0 = s0_ref[d].T
        else:
            st0 = jnp.zeros((V, K), F32)

        def body(i, st, d=d, keep=keep, tri=tri, a2=a2, ab=ab):
            n = i if d == 0 else N - 1 - i
            sl = pl.ds(pl.multiple_of(n * C, C), C)
            logit = jnp.dot(ag_ref[sl, :][:, :r2], a2, preferred_element_type=F32, precision=hi) + ab
            la = _log_sigmoid(logit) * (1.0 / GLA_TAU)
            bc = jnp.dot(tri, la, preferred_element_type=F32, precision=hi)
            blast = bc[C - 1:C, :] if d == 0 else bc[0:1, :]
            qc = q_ref[sl, :] * (K ** -0.5)
            kc = k_ref[sl, :]
            vc = v_ref[sl, :].astype(BF16)
            q_in = (qc * jnp.exp(bc)).astype(BF16)
            k_in = (kc * jnp.exp(-bc)).astype(BF16)
            k_out = (kc * jnp.exp(blast - bc)).astype(BF16)
            att = lax.dot_general(q_in, k_in, nt, preferred_element_type=F32)
            att = jnp.where(keep, att, 0.0).astype(BF16)
            o = jnp.dot(att, vc, preferred_element_type=F32)
            o = o + lax.dot_general(q_in, st.astype(BF16), nt, preferred_element_type=F32)
            if d == 0:
                oacc_ref[sl, :] = o
            else:
                oacc_ref[sl, :] += o
            ds_t = lax.dot_general(vc, k_out, tn, preferred_element_type=F32)
            return jnp.exp(blast) * st + ds_t

        st = lax.fori_loop(0, N, body, st0)
        if want_state:
            sfin_ref[d] = st.T

    og = _rms_rows(oacc_ref[...], nw_ref[...]) * _silu(r_ref[...])
    o_ref[...] = og.astype(o_ref.dtype)


def _gla(u, ag, row0, a2p, ab, norm_w, s0, *, n_seq, L, heads, K, V, qcol, vcol, rcol, want_state):
    assert row0 % L == 0 and qcol % K == 0 and vcol % V == 0 and rcol % V == 0
    rb0 = row0 // L
    qb, kb, vb, rb = qcol // K, qcol // K + heads, vcol // V, rcol // V
    r2 = a2p.shape[1]
    has_s0 = s0 is not None
    kern = functools.partial(_gla_kernel, L=L, K=K, V=V, has_s0=has_s0, want_state=want_state)
    in_specs = [
        pl.BlockSpec((L, K), lambda b, h: (rb0 + b, qb + h)),
        pl.BlockSpec((L, K), lambda b, h: (rb0 + b, kb + h)),
        pl.BlockSpec((L, V), lambda b, h: (rb0 + b, vb + h)),
        pl.BlockSpec((L, V), lambda b, h: (rb0 + b, rb + h)),
        pl.BlockSpec((L, LANE), lambda b, h: (rb0 + b, 0)),
        pl.BlockSpec((2, r2, K), lambda b, h: (0, 0, h)),
        pl.BlockSpec((2, 1, K), lambda b, h: (0, 0, h)),
        pl.BlockSpec((1, V), lambda b, h: (0, 0)),
    ]
    args = [u, u, u, u, ag, a2p, ab, norm_w]
    if has_s0:
        in_specs.append(pl.BlockSpec((None, 2, None, K, V), lambda b, h: (b, 0, h, 0, 0)))
        args.append(s0)
    out_specs = [pl.BlockSpec((L, V), lambda b, h: (b, h))]
    out_shape = [jax.ShapeDtypeStruct((n_seq * L, heads * V), BF16)]
    if want_state:
        out_specs.append(pl.BlockSpec((None, 2, None, K, V), lambda b, h: (b, 0, h, 0, 0)))
        out_shape.append(jax.ShapeDtypeStruct((n_seq, 2, heads, K, V), F32))
    res = pl.pallas_call(
        kern,
        grid=(n_seq, heads),
        in_specs=in_specs,
        out_specs=out_specs,
        out_shape=out_shape,
        scratch_shapes=[pltpu.VMEM((L, V), F32)],
        compiler_params=_cparams(2),
    )(*args)
    return res if want_state else (res[0], None)


def _merge_kernel(od_ref, og_ref, wd_ref, wg_ref, gd_ref, gg_ref, bd_ref, bg_ref, o_ref, wdb_ref, wgb_ref):
    @pl.when(pl.program_id(1) == 0)
    def _():
        wdb_ref[...] = wd_ref[...].astype(BF16)
        wgb_ref[...] = wg_ref[...].astype(BF16)

    yd = jnp.dot(od_ref[...], wdb_ref[...], preferred_element_type=F32)
    yg = jnp.dot(og_ref[...], wgb_ref[...], preferred_element_type=F32)
    m = _sigmoid(gd_ref[...] + bd_ref[...]) * yd + _sigmoid(gg_ref[...] + bg_ref[...]) * yg
    o_ref[...] = m.astype(o_ref.dtype)


def _merge(od, og, w_da, w_gla, gates, b_gate4, l, *, tm=512, tn=512):
    T, Kd = od.shape
    Kg = og.shape[1]
    D = w_da.shape[2]
    tm, tn = _tile(T, tm), _tile(D, tn)
    nj = D // tn
    return pl.pallas_call(
        _merge_kernel,
        grid=(nj, T // tm),
        in_specs=[
            pl.BlockSpec((tm, Kd), lambda j, i: (i, 0)),
            pl.BlockSpec((tm, Kg), lambda j, i: (i, 0)),
            pl.BlockSpec((None, Kd, tn), lambda j, i: (l, 0, j)),
            pl.BlockSpec((None, Kg, tn), lambda j, i: (l, 0, j)),
            pl.BlockSpec((tm, tn), lambda j, i: (i, j)),
            pl.BlockSpec((tm, tn), lambda j, i: (i, nj + j)),
            pl.BlockSpec((None, None, 1, tn), lambda j, i: (l, 0, 0, j)),
            pl.BlockSpec((None, None, 1, tn), lambda j, i: (l, 1, 0, j)),
        ],
        out_specs=pl.BlockSpec((tm, tn), lambda j, i: (i, j)),
        out_shape=jax.ShapeDtypeStruct((T, D), BF16),
        scratch_shapes=[pltpu.VMEM((Kd, tn), BF16), pltpu.VMEM((Kg, tn), BF16)],
        compiler_params=_cparams(2),
    )(od, og, w_da, w_gla, gates, gates, b_gate4, b_gate4)


def _layer_norm_rows(z, g, b):
    mu = jnp.mean(z, axis=-1, keepdims=True)
    zc = z - mu
    var = jnp.mean(zc * zc, axis=-1, keepdims=True)
    return zc * lax.rsqrt(var + LN_EPS) * g + b


def _post_mixer_kernel(x_ref, y_ref, g1_ref, sc_ref, sh_ref, lg_ref, lb_ref, wr_ref, br_ref,
                       x1_ref, h2_ref, lo_ref, *, alpha):
    x1 = _layer_norm_rows(alpha * x_ref[...] + g1_ref[...] * y_ref[...], lg_ref[...], lb_ref[...])
    x1_ref[...] = x1
    h2 = x1 * (1.0 + sc_ref[...]) + sh_ref[...]
    h2_ref[...] = h2
    lo_ref[...] = jnp.dot(h2, wr_ref[...], preferred_element_type=F32,
                          precision=lax.Precision.HIGHEST) + br_ref[...]


def _post_mixer(x, y, mod5, slab, l, ln_g, ln_b, wr, br, *, alpha):
    T, D = x.shape
    tm = slab.tile(256)
    nr = wr.shape[2]
    row = pl.BlockSpec((tm, D), lambda i: (i, 0))
    vec = pl.BlockSpec((None, 1, D), lambda i: (l, 0, 0))
    return pl.pallas_call(
        functools.partial(_post_mixer_kernel, alpha=alpha),
        grid=(T // tm,),
        in_specs=[row, row, slab.mod_spec(tm, l, 2, D), slab.mod_spec(tm, l, 4, D), slab.mod_spec(tm, l, 3, D),
                  vec, vec,
                  pl.BlockSpec((None, D, nr), lambda i: (l, 0, 0)),
                  pl.BlockSpec((None, 1, nr), lambda i: (l, 0, 0))],
        out_specs=[row, row, pl.BlockSpec((tm, nr), lambda i: (i, 0))],
        out_shape=[jax.ShapeDtypeStruct((T, D), F32), jax.ShapeDtypeStruct((T, D), F32),
                   jax.ShapeDtypeStruct((T, nr), F32)],
        compiler_params=_cparams(1),
    )(x, y, mod5, mod5, mod5, ln_g, ln_b, wr, br)


def _moe_kernel(blk_e_ref, nused_ref, tok_ref, x_hbm, w1_ref, w3_ref, w2_ref, rw_ref, o_ref,
                xbuf_ref, xb16_ref, sem, *, tm, nf):
    b = pl.program_id(0)
    f = pl.program_id(1)

    def row_copy(r):
        t = tok_ref[b * tm + r]
        return pltpu.make_async_copy(x_hbm.at[pl.ds(t, 1), :], xbuf_ref.at[pl.ds(r, 1), :], sem)

    @pl.when(b < nused_ref[0])
    def _():
        @pl.when(f == 0)
        def _():
            def issue(r, c):
                row_copy(r).start()
                return c

            def drain(r, c):
                row_copy(r).wait()
                return c

            lax.fori_loop(0, tm, issue, 0)
            lax.fori_loop(0, tm, drain, 0)
            xb16_ref[...] = xbuf_ref[...].astype(BF16)

        xb = xb16_ref[...]
        h1 = jnp.dot(xb, w1_ref[...].astype(BF16), preferred_element_type=F32)
        h3 = jnp.dot(xb, w3_ref[...].astype(BF16), preferred_element_type=F32)
        hid = (_silu(h1) * h3).astype(BF16)
        part = jnp.dot(hid, w2_ref[...].astype(BF16), preferred_element_type=F32)

        @pl.when(f == 0)
        def _():
            o_ref[...] = part

        @pl.when(f > 0)
        def _():
            o_ref[...] += part

        @pl.when(f == nf - 1)
        def _():
            o_ref[...] = o_ref[...] * rw_ref[...]

    @pl.when(b >= nused_ref[0])
    def _():
        @pl.when(f == 0)
        def _():
            o_ref[...] = jnp.zeros_like(o_ref)


def _moe(h2, blk_e, nused, buf_tok, buf_w, w1, w3, w2, l, *, tm, tf=256):
    T, D = h2.shape
    FF = w1.shape[3]
    tf = _tile(FF, tf)
    nf = FF // tf
    P = buf_tok.shape[0]
    nb = P // tm
    grid_spec = pltpu.PrefetchScalarGridSpec(
        num_scalar_prefetch=3,
        grid=(nb, nf),
        in_specs=[
            pl.BlockSpec(memory_space=pl.ANY),
            pl.BlockSpec((None, None, D, tf), lambda b, f, be, nu, tk: (l, be[b], 0, f)),
            pl.BlockSpec((None, None, D, tf), lambda b, f, be, nu, tk: (l, be[b], 0, f)),
            pl.BlockSpec((None, None, tf, D), lambda b, f, be, nu, tk: (l, be[b], f, 0)),
            pl.BlockSpec((tm, 1), lambda b, f, be, nu, tk: (b, 0)),
        ],
        out_specs=pl.BlockSpec((tm, D), lambda b, f, be, nu, tk: (b, 0)),
        scratch_shapes=[pltpu.VMEM((tm, D), F32), pltpu.VMEM((tm, D), BF16), pltpu.SemaphoreType.DMA(())],
    )
    return pl.pallas_call(
        functools.partial(_moe_kernel, tm=tm, nf=nf),
        grid_spec=grid_spec,
        out_shape=jax.ShapeDtypeStruct((P, D), F32),
        compiler_params=_cparams(2),
    )(blk_e, nused, buf_tok, h2, w1, w3, w2, buf_w.reshape(P, 1))


def _post_moe_kernel(p0_ref, p1_ref, es_hbm, x_ref, g2_ref, lg_ref, lb_ref, *rest, tm, alpha, has_next):
    if has_next:
        sc_ref, sh_ref, x2_ref, h_ref, buf0, buf1, sem = rest
    else:
        x2_ref, buf0, buf1, sem = rest
    i = pl.program_id(0)

    def copies(r):
        a = pltpu.make_async_copy(es_hbm.at[pl.ds(p0_ref[i * tm + r], 1), :], buf0.at[pl.ds(r, 1), :], sem.at[0])
        c = pltpu.make_async_copy(es_hbm.at[pl.ds(p1_ref[i * tm + r], 1), :], buf1.at[pl.ds(r, 1), :], sem.at[1])
        return a, c

    def issue(r, c):
        a, b = copies(r)
        a.start()
        b.start()
        return c

    def drain(r, c):
        a, b = copies(r)
        a.wait()
        b.wait()
        return c

    lax.fori_loop(0, tm, issue, 0)
    lax.fori_loop(0, tm, drain, 0)
    y = buf0[...] + buf1[...]
    x2 = _layer_norm_rows(alpha * x_ref[...] + g2_ref[...] * y, lg_ref[...], lb_ref[...])
    x2_ref[...] = x2
    if has_next:
        h_ref[...] = (x2 * (1.0 + sc_ref[...]) + sh_ref[...]).astype(h_ref.dtype)


def _post_moe(p0, p1, es, x1, mod5, slab, l, ln_g, ln_b, *, alpha, has_next):
    T, D = x1.shape
    tm = slab.tile(256)
    row = pl.BlockSpec((tm, D), lambda i, *_: (i, 0))
    vec = pl.BlockSpec((None, 1, D), lambda i, *_: (l, 0, 0))
    in_specs = [pl.BlockSpec(memory_space=pl.ANY), row, slab.mod_spec(tm, l, 5, D), vec, vec]
    args = [es, x1, mod5, ln_g, ln_b]
    out_specs = [row]
    out_shape = [jax.ShapeDtypeStruct((T, D), F32)]
    if has_next:
        in_specs += [slab.mod_spec(tm, l + 1, 1, D), slab.mod_spec(tm, l + 1, 0, D)]
        args += [mod5, mod5]
        out_specs.append(row)
        out_shape.append(jax.ShapeDtypeStruct((T, D), BF16))
    grid_spec = pltpu.PrefetchScalarGridSpec(
        num_scalar_prefetch=2,
        grid=(T // tm,),
        in_specs=in_specs,
        out_specs=out_specs,
        scratch_shapes=[pltpu.VMEM((tm, D), F32), pltpu.VMEM((tm, D), F32), pltpu.SemaphoreType.DMA((2,))],
    )
    res = pl.pallas_call(
        functools.partial(_post_moe_kernel, tm=tm, alpha=alpha, has_next=has_next),
        grid_spec=grid_spec,
        out_shape=out_shape,
        compiler_params=_cparams(1),
    )(p0, p1, *args)
    return (res[0], res[1]) if has_next else (res[0], None)


def _route(logits, n_groups, epg, tm):
    T = logits.shape[0]
    n_exp = n_groups * epg
    pg = jax.nn.softmax(logits[:, :n_groups], axis=-1)
    g_idx = jnp.argmax(pg, axis=-1)
    p_sel = jnp.take_along_axis(pg, g_idx[:, None], axis=-1)
    le = logits[:, n_groups:n_groups + n_exp].reshape(T, n_groups, epg)
    le = jnp.take_along_axis(le, g_idx[:, None, None], axis=1)[:, 0]
    pe = jax.nn.softmax(le, axis=-1)
    top_p, top_i = lax.top_k(pe, TOP_K)
    gate = p_sel * top_p / jnp.sum(top_p, axis=-1, keepdims=True)
    expert = (g_idx[:, None] * epg + top_i).astype(jnp.int32)

    A = T * TOP_K
    e_flat = expert.reshape(A)
    w_flat = gate.reshape(A)
    tok_flat = jnp.repeat(jnp.arange(T, dtype=jnp.int32), TOP_K)
    onehot = (e_flat[:, None] == jnp.arange(n_exp, dtype=jnp.int32)[None, :]).astype(jnp.int32)
    csum = jnp.cumsum(onehot, axis=0)
    counts = csum[-1]
    rank = jnp.take_along_axis(csum, e_flat[:, None], axis=1)[:, 0] - 1
    padded = (counts + tm - 1) // tm * tm
    pends = jnp.cumsum(padded)
    pstarts = pends - padded
    pos = (pstarts[e_flat] + rank).astype(jnp.int32)
    n_blocks = -(-A // tm) + n_exp
    P = n_blocks * tm
    buf_tok = jnp.zeros((P,), jnp.int32).at[pos].set(tok_flat)
    buf_w = jnp.zeros((P,), F32).at[pos].set(w_flat)
    blk_e = jnp.clip(jnp.searchsorted(pends, jnp.arange(n_blocks, dtype=pends.dtype) * tm, side='right'),
                     0, n_exp - 1).astype(jnp.int32)
    nused = (pends[-1] // tm).astype(jnp.int32).reshape(1)
    pos2 = pos.reshape(T, TOP_K)
    return blk_e, nused, buf_tok, buf_w, pos2[:, 0], pos2[:, 1]


def _rope_tables(n_tokens, dk):
    freqs = dk // 4
    rows = n_tokens // GRID_W
    row = jnp.repeat(jnp.arange(rows, dtype=F32), GRID_W)
    col = jnp.tile(jnp.arange(GRID_W, dtype=F32), rows)
    inv = 1.0 / (ROPE_THETA ** (jnp.arange(freqs, dtype=F32) / freqs))
    ar, ac = row[:, None] * inv, col[:, None] * inv
    z = jnp.zeros_like(ar)
    cos = jnp.concatenate([jnp.cos(ar), jnp.cos(ar), jnp.cos(ac), jnp.cos(ac)], axis=1)
    s_lo = jnp.concatenate([-jnp.sin(ar), z, -jnp.sin(ac), z], axis=1)
    s_hi = jnp.concatenate([z, jnp.sin(ar), z, jnp.sin(ac)], axis=1)
    return cos, s_lo, s_hi


def kernel(x_prompt, x_sample, cache_k, cache_v, state_gla, c, c_ctx, w_mod, b_mod, w_in, b_gate, da_lambda,
           da_norm_w, gla_a2, gla_a_bias, gla_norm_w, w_da_proj, w_gla_proj, w_out, ln1_g, ln1_b, ln2_g, ln2_b,
           router_g_w, router_g_b, router_e_w, router_e_b, exp_w1, exp_w3, exp_w2):
    B, S, D = x_prompt.shape
    n_dec, dec_seq, _ = x_sample.shape
    depth = w_mod.shape[0]
    past, da_heads, _, dk = cache_k.shape[2:]
    dv = cache_v.shape[-1]
    gla_heads, gk, gv = state_gla.shape[3:]
    lowrank = gla_a2.shape[2]
    n_groups, epg = router_e_w.shape[2:]
    n_exp = n_groups * epg
    da_qw, da_vw = da_heads * 2 * dk, da_heads * dv
    gla_qw, gla_vw = gla_heads * gk, gla_heads * gv
    n_main = 2 * da_qw + da_vw + 2 * gla_qw + 2 * gla_vw
    assert n_main % LANE == 0 and w_in.shape[2] == n_main + 2 * lowrank + 2 * D
    alpha = (2.0 * depth) ** 0.25

    slab = _Slab(B * S, dec_seq, n_dec)
    t_ctx, T = slab.t_ctx, slab.T
    moe_tm = 256

    x = jnp.concatenate([x_prompt.reshape(t_ctx, D), x_sample.reshape(n_dec * dec_seq, D)], axis=0)
    n_cond = 8
    cond = jnp.zeros((n_cond, D), F32).at[0].set(c_ctx).at[1:1 + n_dec].set(c)
    w_gates = w_in[:, :, n_main + 2 * lowrank:]
    b_gate4 = b_gate.reshape(depth, 2, 1, D)
    nr = -(-(n_groups + n_exp) // LANE) * LANE
    wr = jnp.concatenate([router_g_w, router_e_w.reshape(depth, D, n_exp),
                          jnp.zeros((depth, D, nr - n_groups - n_exp), F32)], axis=2)
    br = jnp.concatenate([router_g_b, router_e_b.reshape(depth, n_exp),
                          jnp.zeros((depth, nr - n_groups - n_exp), F32)], axis=1).reshape(depth, 1, nr)
    a2p = jnp.zeros((depth, 2, 2 * lowrank, gla_qw), F32)
    a2p = a2p.at[:, 0, :lowrank].set(gla_a2[:, 0]).at[:, 1, lowrank:].set(gla_a2[:, 1])
    ab = gla_a_bias.reshape(depth, 2, 1, gla_qw)
    lv = da_lambda.astype(F32)
    lam_init = [0.8 - 0.6 * math.exp(-0.3 * l) for l in range(depth)]
    lam = [(jnp.exp(jnp.sum(lv[l, 0] * lv[l, 1])) - jnp.exp(jnp.sum(lv[l, 2] * lv[l, 3])) + lam_init[l]).reshape(1, 1)
           for l in range(depth)]
    rope = _rope_tables(dec_seq, dk)
    cache_k2 = cache_k.reshape(n_dec, depth, past, da_qw)
    cache_v2 = cache_v.reshape(n_dec, depth, past, da_vw)
    ln1_g3, ln1_b3 = ln1_g.reshape(depth, 1, D), ln1_b.reshape(depth, 1, D)
    ln2_g3, ln2_b3 = ln2_g.reshape(depth, 1, D), ln2_b.reshape(depth, 1, D)

    mod = _modulation(cond, w_mod, b_mod)
    mod5 = mod.reshape(depth, n_cond, 6, 1, D)

    h = _premod(x, mod5, slab, 0)
    ks, vs, sts = [], [], []
    for l in range(depth):
        u = _matmul(h, w_in, l, 0, n_main)
        ag = _matmul(h, w_in, l, n_main, LANE, tn=LANE)
        gates = _matmul(h, w_gates, l, 0, 2 * D)
        ks.append(u[:t_ctx, da_qw:2 * da_qw].reshape(B, S, da_heads, 2, dk))
        vs.append(u[:t_ctx, 2 * da_qw:2 * da_qw + da_vw].reshape(B, S, da_heads, dv))

        post = 1.0 - lam_init[l]
        nw_d = da_norm_w[l].reshape(1, dv)
        od_c = _attn_ctx(u, lam[l], nw_d, n_seq=B, L=S, heads=da_heads, dk=dk, dv=dv, post=post)
        od_d = _attn_dec(u, t_ctx, cache_k2, cache_v2, l, rope, lam[l], nw_d,
                         n_seq=n_dec, L=dec_seq, heads=da_heads, dk=dk, dv=dv, post=post)
        od = jnp.concatenate([od_c, od_d], axis=0)

        qcol = 2 * da_qw + da_vw
        vcol = qcol + 2 * gla_qw
        rcol = vcol + gla_vw
        nw_g = gla_norm_w[l].reshape(1, gv)
        gla_kw = dict(heads=gla_heads, K=gk, V=gv, qcol=qcol, vcol=vcol, rcol=rcol)
        og_c, st = _gla(u, ag, 0, a2p[l], ab[l], nw_g, None, n_seq=B, L=S, want_state=True, **gla_kw)
        og_d, _ = _gla(u, ag, t_ctx, a2p[l], ab[l], nw_g, state_gla[:, l], n_seq=n_dec, L=dec_seq,
                       want_state=False, **gla_kw)
        og = jnp.concatenate([og_c, og_d], axis=0)
        sts.append(st)

        m = _merge(od, og, w_da_proj, w_gla_proj, gates, b_gate4, l)
        y = _matmul(m, w_out, l, 0, D)
        x1, h2, logits = _post_mixer(x, y, mod5, slab, l, ln1_g3, ln1_b3, wr, br, alpha=alpha)

        blk_e, nused, buf_tok, buf_w, p0, p1 = _route(logits, n_groups, epg, moe_tm)
        es = _moe(h2, blk_e, nused, buf_tok, buf_w, exp_w1, exp_w3, exp_w2, l, tm=moe_tm)
        x, h = _post_moe(p0, p1, es, x1, mod5, slab, l, ln2_g3, ln2_b3, alpha=alpha, has_next=l + 1 < depth)

    y_prompt = x[:t_ctx].reshape(B, S, D)
    y_sample = x[t_ctx:].reshape(n_dec, dec_seq, D)
    return (y_prompt, y_sample, jnp.stack(ks, axis=1), jnp.stack(vs, axis=1), jnp.stack(sts, axis=1))
```

```python
import functools
import math

import jax
import jax.numpy as jnp
from jax import lax
from jax.experimental import pallas as pl
from jax.experimental.pallas import tpu as pltpu

F32 = jnp.float32
BF16 = jnp.bfloat16

GRID_W = 64
ROPE_THETA = 10000.0
GLA_TAU = 16.0
GLA_CHUNK = 64
TOP_K = 2
LN_EPS = 1e-5
LANE = 128
VMEM_LIMIT = 56 * 1024 * 1024
DMA_UNROLL = 8


def _cparams(n_axes):
    return pltpu.CompilerParams(dimension_semantics=("arbitrary",) * n_axes, vmem_limit_bytes=VMEM_LIMIT)


def _tile(n, pref):
    t = min(n, pref)
    while n % t:
        t //= 2
    return t


def _silu(x):
    return x * (1.0 / (1.0 + jnp.exp(-x)))


def _sigmoid(x):
    return 1.0 / (1.0 + jnp.exp(-x))


def _mm_kernel(x_ref, w_ref, o_ref, wb_ref):
    @pl.when(pl.program_id(1) == 0)
    def _():
        wb_ref[...] = w_ref[...].astype(BF16)

    o_ref[...] = jnp.dot(x_ref[...], wb_ref[...], preferred_element_type=F32).astype(o_ref.dtype)


def _mm_shift_kernel(x_ref, wa_ref, wt_ref, o_ref, wb_ref, *, shift):
    @pl.when(pl.program_id(1) == 0)
    def _():
        wb_ref[...] = jnp.concatenate([wa_ref[:, shift:], wt_ref[:, :shift]], axis=1).astype(BF16)

    o_ref[...] = jnp.dot(x_ref[...], wb_ref[...], preferred_element_type=F32).astype(o_ref.dtype)


def _matmul(x, w, l, col0, ncols, *, tm=1024, tn=512, out_dtype=F32):
    M, K = x.shape
    tm = _tile(M, tm)
    tn = _tile(ncols, tn)
    base = col0 // LANE * LANE
    shift = col0 - base
    assert base % tn == 0 and tn % LANE == 0
    cb0 = base // tn
    in_specs = [
        pl.BlockSpec((tm, K), lambda j, i: (i, 0)),
        pl.BlockSpec((None, K, tn), lambda j, i: (l, 0, cb0 + j)),
    ]
    args = [x, w]
    kern = _mm_kernel
    if shift:
        lpt = tn // LANE
        in_specs.append(pl.BlockSpec((None, K, LANE), lambda j, i: (l, 0, (cb0 + j + 1) * lpt)))
        args.append(w)
        kern = functools.partial(_mm_shift_kernel, shift=shift)
    return pl.pallas_call(
        kern,
        grid=(ncols // tn, M // tm),
        in_specs=in_specs,
        out_specs=pl.BlockSpec((tm, tn), lambda j, i: (i, j)),
        out_shape=jax.ShapeDtypeStruct((M, ncols), out_dtype),
        scratch_shapes=[pltpu.VMEM((K, tn), BF16)],
        compiler_params=_cparams(2),
    )(*args)


def _mod_kernel(c_ref, w_ref, b_ref, o_ref):
    a = _silu(c_ref[...]).astype(BF16)
    o_ref[...] = jnp.dot(a, w_ref[...].astype(BF16), preferred_element_type=F32) + b_ref[...]


def _modulation(cond, w_mod, b_mod, *, tn=512):
    depth, D, N = w_mod.shape
    R = cond.shape[0]
    tn = _tile(N, tn)
    return pl.pallas_call(
        _mod_kernel,
        grid=(depth, N // tn),
        in_specs=[
            pl.BlockSpec((R, D), lambda l, j: (0, 0)),
            pl.BlockSpec((None, D, tn), lambda l, j: (l, 0, j)),
            pl.BlockSpec((None, 1, tn), lambda l, j: (l, 0, j)),
        ],
        out_specs=pl.BlockSpec((None, R, tn), lambda l, j: (l, 0, j)),
        out_shape=jax.ShapeDtypeStruct((depth, R, N), F32),
        compiler_params=_cparams(2),
    )(cond, w_mod, b_mod.reshape(depth, 1, N))


class _Slab:
    def __init__(self, t_ctx, dec_seq, n_dec):
        self.t_ctx, self.dec_seq, self.n_dec = t_ctx, dec_seq, n_dec
        self.T = t_ctx + dec_seq * n_dec

    def tile(self, pref):
        t = min(pref, self.t_ctx, self.dec_seq)
        while self.t_ctx % t or self.dec_seq % t:
            t //= 2
        return t

    def mod_spec(self, tm, l, which, D):
        n_ctx = self.t_ctx // tm
        per = self.dec_seq // tm

        def imap(i, *_):
            return (l, jnp.where(i < n_ctx, 0, (i - n_ctx) // per + 1), which, 0, 0)

        return pl.BlockSpec((None, None, None, 1, D), imap)


def _premod_kernel(x_ref, sc_ref, sh_ref, h_ref):
    h_ref[...] = (x_ref[...] * (1.0 + sc_ref[...]) + sh_ref[...]).astype(h_ref.dtype)


def _premod(x, mod5, slab, l):
    T, D = x.shape
    tm = slab.tile(256)
    return pl.pallas_call(
        _premod_kernel,
        grid=(T // tm,),
        in_specs=[pl.BlockSpec((tm, D), lambda i: (i, 0)), slab.mod_spec(tm, l, 1, D), slab.mod_spec(tm, l, 0, D)],
        out_specs=pl.BlockSpec((tm, D), lambda i: (i, 0)),
        out_shape=jax.ShapeDtypeStruct((T, D), BF16),
        compiler_params=_cparams(1),
    )(x, mod5, mod5)


def _softmax_rows(s):
    m = jnp.max(s, axis=-1, keepdims=True)
    e = jnp.exp(s - m)
    return e / jnp.sum(e, axis=-1, keepdims=True)


def _diff_attn_rows(q, k1, k2, v, lam, scale, dk):
    nt = (((1,), (1,)), ((), ()))
    s1 = lax.dot_general(q[:, :dk].astype(BF16), k1, nt, preferred_element_type=F32) * scale
    s2 = lax.dot_general(q[:, dk:].astype(BF16), k2, nt, preferred_element_type=F32) * scale
    a = _softmax_rows(s1) - lam * _softmax_rows(s2)
    return jnp.dot(a.astype(BF16), v, preferred_element_type=F32)


def _rms_rows(o, w):
    return o * lax.rsqrt(jnp.mean(o * o, axis=-1, keepdims=True) + LN_EPS) * w


def _attn_ctx_kernel(lam_ref, q_ref, k_ref, v_ref, nw_ref, o_ref, *, heads, dk, dv, post):
    lam = lam_ref[0, 0]
    scale = dk ** -0.5
    for h in range(heads):
        q = q_ref[:, h * 2 * dk:(h + 1) * 2 * dk]
        k = k_ref[:, h * 2 * dk:(h + 1) * 2 * dk].astype(BF16)
        v = v_ref[:, h * dv:(h + 1) * dv].astype(BF16)
        o = _diff_attn_rows(q, k[:, :dk], k[:, dk:], v, lam, scale, dk)
        o_ref[:, h * dv:(h + 1) * dv] = (_rms_rows(o, nw_ref[...]) * post).astype(o_ref.dtype)


def _attn_ctx(u, lam, norm_w, *, n_seq, L, heads, dk, dv, post):
    qw = heads * 2 * dk
    vw = heads * dv
    assert qw == vw
    kern = functools.partial(_attn_ctx_kernel, heads=heads, dk=dk, dv=dv, post=post)
    return pl.pallas_call(
        kern,
        grid=(n_seq,),
        in_specs=[
            pl.BlockSpec(memory_space=pltpu.SMEM),
            pl.BlockSpec((L, qw), lambda b: (b, 0)),
            pl.BlockSpec((L, qw), lambda b: (b, 1)),
            pl.BlockSpec((L, vw), lambda b: (b, 2)),
            pl.BlockSpec((1, dv), lambda b: (0, 0)),
        ],
        out_specs=pl.BlockSpec((L, vw), lambda b: (b, 0)),
        out_shape=jax.ShapeDtypeStruct((n_seq * L, vw), BF16),
        compiler_params=_cparams(1),
    )(lam, u, u, u, norm_w)


def _rope_rows(x, cos, s_lo, s_hi, dk):
    q4 = dk // 4
    return x * cos + pltpu.roll(x, dk - q4, 1) * s_lo + pltpu.roll(x, q4, 1) * s_hi


def _attn_dec_kernel(lam_ref, q_ref, k_ref, v_ref, ck_ref, cv_ref, cos_ref, slo_ref, shi_ref, nw_ref, o_ref,
                     kb_ref, vb_ref, *, dk, dv, past, L, tq, post):
    lam = lam_ref[0, 0]
    scale = dk ** -0.5
    cos, slo, shi = cos_ref[...], slo_ref[...], shi_ref[...]
    kb_ref[0:past, :] = ck_ref[...].astype(BF16)
    vb_ref[0:past, :] = cv_ref[...].astype(BF16)
    for c in range(2):
        kb_ref[past:past + L, c * dk:(c + 1) * dk] = _rope_rows(
            k_ref[:, c * dk:(c + 1) * dk], cos, slo, shi, dk).astype(BF16)
    vb_ref[past:past + L, :] = v_ref[...].astype(BF16)
    k1 = kb_ref[:, :dk]
    k2 = kb_ref[:, dk:]
    v = vb_ref[...]
    for i in range(L // tq):
        rows = slice(i * tq, (i + 1) * tq)
        q = jnp.concatenate(
            [_rope_rows(q_ref[rows, c * dk:(c + 1) * dk], cos[rows], slo[rows], shi[rows], dk) for c in range(2)],
            axis=1)
        o = _diff_attn_rows(q, k1, k2, v, lam, scale, dk)
        o_ref[rows, :] = (_rms_rows(o, nw_ref[...]) * post).astype(o_ref.dtype)


def _attn_dec(u, row0, cache_k, cache_v, l, rope, lam, norm_w, *, n_seq, L, heads, dk, dv, post):
    past = cache_k.shape[2]
    assert row0 % L == 0
    rb0 = row0 // L
    tq = _tile(L, 256)
    kern = functools.partial(_attn_dec_kernel, dk=dk, dv=dv, past=past, L=L, tq=tq, post=post)
    cos, slo, shi = rope
    tab = pl.BlockSpec((L, dk), lambda b, h: (0, 0))
    return pl.pallas_call(
        kern,
        grid=(n_seq, heads),
        in_specs=[
            pl.BlockSpec(memory_space=pltpu.SMEM),
            pl.BlockSpec((L, 2 * dk), lambda b, h: (rb0 + b, h)),
            pl.BlockSpec((L, 2 * dk), lambda b, h: (rb0 + b, heads + h)),
            pl.BlockSpec((L, dv), lambda b, h: (rb0 + b, 2 * heads + h)),
            pl.BlockSpec((None, None, past, 2 * dk), lambda b, h: (b, l, 0, h)),
            pl.BlockSpec((None, None, past, dv), lambda b, h: (b, l, 0, h)),
            tab, tab, tab,
            pl.BlockSpec((1, dv), lambda b, h: (0, 0)),
        ],
        out_specs=pl.BlockSpec((L, dv), lambda b, h: (b, h)),
        out_shape=jax.ShapeDtypeStruct((n_seq * L, heads * dv), BF16),
        scratch_shapes=[pltpu.VMEM((past + L, 2 * dk), BF16), pltpu.VMEM((past + L, dv), BF16)],
        compiler_params=_cparams(2),
    )(lam, u, u, u, cache_k, cache_v, cos, slo, shi, norm_w)


def _log_sigmoid(x):
    return -(jnp.maximum(-x, 0.0) + jnp.log1p(jnp.exp(-jnp.abs(x))))


def _split3_bf16(x):
    hi = x.astype(BF16)
    r1 = x - hi.astype(F32)
    mid = r1.astype(BF16)
    lo = (r1 - mid.astype(F32)).astype(BF16)
    return hi, mid, lo


def _gla_kernel(*refs, L, K, V, G, has_s0, want_state):
    q_ref, k_ref, v_ref, r_ref, ag_ref, a2_ref, ab_ref, nw_ref = refs[:8]
    refs = refs[8:]
    s0_ref = None
    if has_s0:
        s0_ref, refs = refs[0], refs[1:]
    o_ref, refs = refs[0], refs[1:]
    sfin_ref = None
    if want_state:
        sfin_ref, refs = refs[0], refs[1:]
    st_ref, of_ref, ob_ref = refs

    C = GLA_CHUNK
    N = L // C
    r2 = a2_ref.shape[1]
    row = lax.broadcasted_iota(jnp.int32, (C, C), 0)
    col = lax.broadcasted_iota(jnp.int32, (C, C), 1)
    nt = (((1,), (1,)), ((), ()))
    tn = (((0,), (0,)), ((), ()))
    keeps = (col <= row, col >= row)
    tris = tuple(jnp.where(kp, 1.0, 0.0).astype(BF16) for kp in keeps)

    for d in range(2):
        for g in range(G):
            st_ref[d, g] = s0_ref[d, g].T if has_s0 else jnp.zeros((V, K), F32)

    def body(i, carry):
        for d in range(2):
            n = i if d == 0 else N - 1 - i
            sl = pl.ds(pl.multiple_of(n * C, C), C)
            agc = ag_ref[sl, :][:, :r2].astype(BF16)
            logit = jnp.dot(agc, a2_ref[d].astype(BF16), preferred_element_type=F32) + ab_ref[d]
            la = _log_sigmoid(logit) * (1.0 / GLA_TAU)
            hi, mid, lo = _split3_bf16(la)
            bc = (jnp.dot(tris[d], hi, preferred_element_type=F32) + jnp.dot(tris[d], mid, preferred_element_type=F32)
                  + jnp.dot(tris[d], lo, preferred_element_type=F32))
            blast = bc[C - 1:C, :] if d == 0 else bc[0:1, :]
            kc = k_ref[sl, :]
            q_in_all = (q_ref[sl, :] * (K ** -0.5) * jnp.exp(bc)).astype(BF16)
            k_in_all = (kc * jnp.exp(-bc)).astype(BF16)
            k_out_all = (kc * jnp.exp(blast - bc)).astype(BF16)
            dec_all = jnp.exp(blast)
            for g in range(G):
                ks = slice(g * K, (g + 1) * K)
                vs = slice(g * V, (g + 1) * V)
                q_in = q_in_all[:, ks]
                vc = v_ref[sl, vs].astype(BF16)
                att = lax.dot_general(q_in, k_in_all[:, ks], nt, preferred_element_type=F32)
                att = jnp.where(keeps[d], att, 0.0).astype(BF16)
                st = st_ref[d, g]
                o = jnp.dot(att, vc, preferred_element_type=F32)
                o = o + lax.dot_general(q_in, st.astype(BF16), nt, preferred_element_type=F32)
                (of_ref if d == 0 else ob_ref)[sl, vs] = o
                ds_t = lax.dot_general(vc, k_out_all[:, ks], tn, preferred_element_type=F32)
                st_ref[d, g] = dec_all[:, ks] * st + ds_t
        return carry

    lax.fori_loop(0, N, body, 0)
    if want_state:
        for d in range(2):
            for g in range(G):
                sfin_ref[d, g] = st_ref[d, g].T
    for g in range(G):
        vs = slice(g * V, (g + 1) * V)
        og = _rms_rows(of_ref[:, vs] + ob_ref[:, vs], nw_ref[...]) * _silu(r_ref[:, vs])
        o_ref[:, vs] = og.astype(o_ref.dtype)


def _gla(u, ag, row0, a2p, ab, norm_w, s0, *, n_seq, L, heads, K, V, qcol, vcol, rcol, want_state):
    G = _tile(heads, 4)
    gk, gv = G * K, G * V
    assert row0 % L == 0 and qcol % gk == 0 and (heads * K) % gk == 0 and vcol % gv == 0 and rcol % gv == 0
    rb0 = row0 // L
    qb, kb, vb, rb = qcol // gk, (qcol + heads * K) // gk, vcol // gv, rcol // gv
    r2 = a2p.shape[1]
    has_s0 = s0 is not None
    kern = functools.partial(_gla_kernel, L=L, K=K, V=V, G=G, has_s0=has_s0, want_state=want_state)
    in_specs = [
        pl.BlockSpec((L, gk), lambda b, h: (rb0 + b, qb + h)),
        pl.BlockSpec((L, gk), lambda b, h: (rb0 + b, kb + h)),
        pl.BlockSpec((L, gv), lambda b, h: (rb0 + b, vb + h)),
        pl.BlockSpec((L, gv), lambda b, h: (rb0 + b, rb + h)),
        pl.BlockSpec((L, LANE), lambda b, h: (rb0 + b, 0)),
        pl.BlockSpec((2, r2, gk), lambda b, h: (0, 0, h)),
        pl.BlockSpec((2, 1, gk), lambda b, h: (0, 0, h)),
        pl.BlockSpec((1, V), lambda b, h: (0, 0)),
    ]
    args = [u, u, u, u, ag, a2p, ab, norm_w]
    if has_s0:
        in_specs.append(pl.BlockSpec((None, 2, G, K, V), lambda b, h: (b, 0, h, 0, 0)))
        args.append(s0)
    out_specs = [pl.BlockSpec((L, gv), lambda b, h: (b, h))]
    out_shape = [jax.ShapeDtypeStruct((n_seq * L, heads * V), BF16)]
    if want_state:
        out_specs.append(pl.BlockSpec((None, 2, G, K, V), lambda b, h: (b, 0, h, 0, 0)))
        out_shape.append(jax.ShapeDtypeStruct((n_seq, 2, heads, K, V), F32))
    res = pl.pallas_call(
        kern,
        grid=(n_seq, heads // G),
        in_specs=in_specs,
        out_specs=out_specs,
        out_shape=out_shape,
        scratch_shapes=[pltpu.VMEM((2, G, V, K), F32), pltpu.VMEM((L, gv), F32), pltpu.VMEM((L, gv), F32)],
        compiler_params=_cparams(2),
    )(*args)
    return res if want_state else (res[0], None)


def _merge_kernel(od_ref, og_ref, wd_ref, wg_ref, gd_ref, gg_ref, bd_ref, bg_ref, o_ref, wdb_ref, wgb_ref):
    @pl.when(pl.program_id(1) == 0)
    def _():
        wdb_ref[...] = wd_ref[...].astype(BF16)
        wgb_ref[...] = wg_ref[...].astype(BF16)

    yd = jnp.dot(od_ref[...], wdb_ref[...], preferred_element_type=F32)
    yg = jnp.dot(og_ref[...], wgb_ref[...], preferred_element_type=F32)
    m = _sigmoid(gd_ref[...] + bd_ref[...]) * yd + _sigmoid(gg_ref[...] + bg_ref[...]) * yg
    o_ref[...] = m.astype(o_ref.dtype)


def _merge(od, og, w_da, w_gla, gates, b_gate4, l, *, tm=1024, tn=512):
    T, Kd = od.shape
    Kg = og.shape[1]
    D = w_da.shape[2]
    tm, tn = _tile(T, tm), _tile(D, tn)
    nj = D // tn
    return pl.pallas_call(
        _merge_kernel,
        grid=(nj, T // tm),
        in_specs=[
            pl.BlockSpec((tm, Kd), lambda j, i: (i, 0)),
            pl.BlockSpec((tm, Kg), lambda j, i: (i, 0)),
            pl.BlockSpec((None, Kd, tn), lambda j, i: (l, 0, j)),
            pl.BlockSpec((None, Kg, tn), lambda j, i: (l, 0, j)),
            pl.BlockSpec((tm, tn), lambda j, i: (i, j)),
            pl.BlockSpec((tm, tn), lambda j, i: (i, nj + j)),
            pl.BlockSpec((None, None, 1, tn), lambda j, i: (l, 0, 0, j)),
            pl.BlockSpec((None, None, 1, tn), lambda j, i: (l, 1, 0, j)),
        ],
        out_specs=pl.BlockSpec((tm, tn), lambda j, i: (i, j)),
        out_shape=jax.ShapeDtypeStruct((T, D), BF16),
        scratch_shapes=[pltpu.VMEM((Kd, tn), BF16), pltpu.VMEM((Kg, tn), BF16)],
        compiler_params=_cparams(2),
    )(od, og, w_da, w_gla, gates, gates, b_gate4, b_gate4)


def _layer_norm_rows(z, g, b):
    mu = jnp.mean(z, axis=-1, keepdims=True)
    zc = z - mu
    var = jnp.mean(zc * zc, axis=-1, keepdims=True)
    return zc * lax.rsqrt(var + LN_EPS) * g + b


def _post_mixer_kernel(x_ref, y_ref, g1_ref, sc_ref, sh_ref, lg_ref, lb_ref, wr_ref, br_ref,
                       x1_ref, h2_ref, lo_ref, *, alpha):
    x1 = _layer_norm_rows(alpha * x_ref[...] + g1_ref[...] * y_ref[...], lg_ref[...], lb_ref[...])
    x1_ref[...] = x1
    h2 = x1 * (1.0 + sc_ref[...]) + sh_ref[...]
    h2_ref[...] = h2
    lo_ref[...] = jnp.dot(h2, wr_ref[...], preferred_element_type=F32,
                          precision=lax.Precision.HIGHEST) + br_ref[...]


def _post_mixer(x, y, mod5, slab, l, ln_g, ln_b, wr, br, *, alpha):
    T, D = x.shape
    tm = slab.tile(256)
    nr = wr.shape[2]
    row = pl.BlockSpec((tm, D), lambda i: (i, 0))
    vec = pl.BlockSpec((None, 1, D), lambda i: (l, 0, 0))
    return pl.pallas_call(
        functools.partial(_post_mixer_kernel, alpha=alpha),
        grid=(T // tm,),
        in_specs=[row, row, slab.mod_spec(tm, l, 2, D), slab.mod_spec(tm, l, 4, D), slab.mod_spec(tm, l, 3, D),
                  vec, vec,
                  pl.BlockSpec((None, D, nr), lambda i: (l, 0, 0)),
                  pl.BlockSpec((None, 1, nr), lambda i: (l, 0, 0))],
        out_specs=[row, row, pl.BlockSpec((tm, nr), lambda i: (i, 0))],
        out_shape=[jax.ShapeDtypeStruct((T, D), F32), jax.ShapeDtypeStruct((T, D), F32),
                   jax.ShapeDtypeStruct((T, nr), F32)],
        compiler_params=_cparams(1),
    )(x, y, mod5, mod5, mod5, ln_g, ln_b, wr, br)


def _issue_rows(copy_fn, tm):
    def f(r, c):
        for cp in copy_fn(r):
            cp.start()
        return c
    lax.fori_loop(0, tm, f, 0, unroll=DMA_UNROLL)


def _wait_rows(src_hbm, dst, sem, tm):
    pltpu.make_async_copy(src_hbm.at[pl.ds(0, tm), :], dst, sem).wait()


def _moe_gather_kernel(nused_ref, tok_ref, x_hbm, o_ref, buf_ref, sem, *, tm):
    b = pl.program_id(0)
    nused = nused_ref[0]

    def issue(blk, slot):
        def copy_fn(r):
            t = tok_ref[blk * tm + r]
            return (pltpu.make_async_copy(x_hbm.at[pl.ds(t, 1), :], buf_ref.at[slot, pl.ds(r, 1), :], sem.at[slot]),)
        _issue_rows(copy_fn, tm)

    @pl.when(jnp.logical_and(b == 0, nused > 0))
    def _():
        issue(0, 0)

    @pl.when(b + 1 < nused)
    def _():
        issue(b + 1, (b + 1) % 2)

    @pl.when(b < nused)
    def _():
        _wait_rows(x_hbm, buf_ref.at[b % 2], sem.at[b % 2], tm)
        o_ref[...] = buf_ref[b % 2].astype(o_ref.dtype)

    @pl.when(b >= nused)
    def _():
        o_ref[...] = jnp.zeros_like(o_ref)


def _moe_gather(h2, nused, buf_tok, *, tm):
    T, D = h2.shape
    P = buf_tok.shape[0]
    grid_spec = pltpu.PrefetchScalarGridSpec(
        num_scalar_prefetch=2,
        grid=(P // tm,),
        in_specs=[pl.BlockSpec(memory_space=pl.ANY)],
        out_specs=pl.BlockSpec((tm, D), lambda b, nu, tk: (b, 0)),
        scratch_shapes=[pltpu.VMEM((2, tm, D), F32), pltpu.SemaphoreType.DMA((2,))],
    )
    return pl.pallas_call(
        functools.partial(_moe_gather_kernel, tm=tm),
        grid_spec=grid_spec,
        out_shape=jax.ShapeDtypeStruct((P, D), BF16),
        compiler_params=_cparams(1),
    )(nused, buf_tok, h2)


def _expert_changed(blk_e_ref, b):
    return jnp.logical_or(b == 0, blk_e_ref[b] != blk_e_ref[jnp.maximum(b - 1, 0)])


def _moe_up_kernel(blk_e_ref, nused_ref, xs_ref, w1_ref, w3_ref, hid_ref, w1b_ref, w3b_ref):
    b = pl.program_id(1)
    used = b < nused_ref[0]

    @pl.when(jnp.logical_and(used, _expert_changed(blk_e_ref, b)))
    def _():
        w1b_ref[...] = w1_ref[...].astype(BF16)
        w3b_ref[...] = w3_ref[...].astype(BF16)

    @pl.when(used)
    def _():
        xs = xs_ref[...]
        h1 = jnp.dot(xs, w1b_ref[...], preferred_element_type=F32)
        h3 = jnp.dot(xs, w3b_ref[...], preferred_element_type=F32)
        hid_ref[...] = (_silu(h1) * h3).astype(hid_ref.dtype)

    @pl.when(jnp.logical_not(used))
    def _():
        hid_ref[...] = jnp.zeros_like(hid_ref)


def _moe_down_kernel(blk_e_ref, nused_ref, hid_ref, w2_ref, o_ref, w2b_ref):
    b = pl.program_id(0)
    used = b < nused_ref[0]

    @pl.when(jnp.logical_and(used, _expert_changed(blk_e_ref, b)))
    def _():
        w2b_ref[...] = w2_ref[...].astype(BF16)

    @pl.when(used)
    def _():
        o_ref[...] = jnp.dot(hid_ref[...], w2b_ref[...], preferred_element_type=F32)

    @pl.when(jnp.logical_not(used))
    def _():
        o_ref[...] = jnp.zeros_like(o_ref)


def _moe(h2, blk_e, nused, buf_tok, w1, w3, w2, l, *, tm, tf=256):
    T, D = h2.shape
    FF = w1.shape[3]
    tf = _tile(FF, tf)
    nf = FF // tf
    P = buf_tok.shape[0]
    nb = P // tm
    xs = _moe_gather(h2, nused, buf_tok, tm=tm)
    up_spec = pltpu.PrefetchScalarGridSpec(
        num_scalar_prefetch=2,
        grid=(nf, nb),
        in_specs=[
            pl.BlockSpec((tm, D), lambda f, b, be, nu: (b, 0)),
            pl.BlockSpec((None, None, D, tf), lambda f, b, be, nu: (l, be[b], 0, f)),
            pl.BlockSpec((None, None, D, tf), lambda f, b, be, nu: (l, be[b], 0, f)),
        ],
        out_specs=pl.BlockSpec((tm, tf), lambda f, b, be, nu: (b, f)),
        scratch_shapes=[pltpu.VMEM((D, tf), BF16), pltpu.VMEM((D, tf), BF16)],
    )
    hid = pl.pallas_call(
        _moe_up_kernel,
        grid_spec=up_spec,
        out_shape=jax.ShapeDtypeStruct((P, FF), BF16),
        compiler_params=_cparams(2),
    )(blk_e, nused, xs, w1, w3)
    down_spec = pltpu.PrefetchScalarGridSpec(
        num_scalar_prefetch=2,
        grid=(nb,),
        in_specs=[
            pl.BlockSpec((tm, FF), lambda b, be, nu: (b, 0)),
            pl.BlockSpec((None, None, FF, D), lambda b, be, nu: (l, be[b], 0, 0)),
        ],
        out_specs=pl.BlockSpec((tm, D), lambda b, be, nu: (b, 0)),
        scratch_shapes=[pltpu.VMEM((FF, D), BF16)],
    )
    return pl.pallas_call(
        _moe_down_kernel,
        grid_spec=down_spec,
        out_shape=jax.ShapeDtypeStruct((P, D), F32),
        compiler_params=_cparams(1),
    )(blk_e, nused, hid, w2)


def _post_moe_kernel(p0_ref, p1_ref, es_hbm, x_ref, gw_ref, g2_ref, lg_ref, lb_ref, *rest, tm, alpha, has_next):
    if has_next:
        sc_ref, sh_ref, x2_ref, h_ref, buf_ref, sem = rest
    else:
        x2_ref, buf_ref, sem = rest
    i = pl.program_id(0)
    n = pl.num_programs(0)

    def issue(tile, slot):
        def copy_fn(r):
            a = pltpu.make_async_copy(es_hbm.at[pl.ds(p0_ref[tile * tm + r], 1), :],
                                      buf_ref.at[slot, 0, pl.ds(r, 1), :], sem.at[slot, 0])
            c = pltpu.make_async_copy(es_hbm.at[pl.ds(p1_ref[tile * tm + r], 1), :],
                                      buf_ref.at[slot, 1, pl.ds(r, 1), :], sem.at[slot, 1])
            return a, c
        _issue_rows(copy_fn, tm)

    @pl.when(i == 0)
    def _():
        issue(0, 0)

    @pl.when(i + 1 < n)
    def _():
        issue(i + 1, (i + 1) % 2)

    slot = i % 2
    for j in range(TOP_K):
        _wait_rows(es_hbm, buf_ref.at[slot, j], sem.at[slot, j], tm)
    y =gw_ref[:, 0:1] * buf_ref[slot, 0] + gw_ref[:, 1:2] * buf_ref[slot, 1]
    x2 = _layer_norm_rows(alpha * x_ref[...] + g2_ref[...] * y, lg_ref[...], lb_ref[...])
    x2_ref[...] = x2
    if has_next:
        h_ref[...] = (x2 * (1.0 + sc_ref[...]) + sh_ref[...]).astype(h_ref.dtype)


def _post_moe(p0, p1, es, x1, gate, mod5, slab, l, ln_g, ln_b, *, alpha, has_next):
    T, D = x1.shape
    tm = slab.tile(256)
    row = pl.BlockSpec((tm, D), lambda i, *_: (i, 0))
    vec = pl.BlockSpec((None, 1, D), lambda i, *_: (l, 0, 0))
    in_specs = [pl.BlockSpec(memory_space=pl.ANY), row, pl.BlockSpec((tm, TOP_K), lambda i, *_: (i, 0)),
                slab.mod_spec(tm, l, 5, D), vec, vec]
    args = [es, x1, gate, mod5, ln_g, ln_b]
    out_specs = [row]
    out_shape = [jax.ShapeDtypeStruct((T, D), F32)]
    if has_next:
        in_specs += [slab.mod_spec(tm, l + 1, 1, D), slab.mod_spec(tm, l + 1, 0, D)]
        args += [mod5, mod5]
        out_specs.append(row)
        out_shape.append(jax.ShapeDtypeStruct((T, D), BF16))
    grid_spec = pltpu.PrefetchScalarGridSpec(
        num_scalar_prefetch=2,
        grid=(T // tm,),
        in_specs=in_specs,
        out_specs=out_specs,
        scratch_shapes=[pltpu.VMEM((2, TOP_K, tm, D), F32), pltpu.SemaphoreType.DMA((2, TOP_K))],
    )
    res = pl.pallas_call(
        functools.partial(_post_moe_kernel, tm=tm, alpha=alpha, has_next=has_next),
        grid_spec=grid_spec,
        out_shape=out_shape,
        compiler_params=_cparams(1),
    )(p0, p1, *args)
    return (res[0], res[1]) if has_next else (res[0], None)


def _route(logits, n_groups, epg, tm):
    T = logits.shape[0]
    n_exp = n_groups * epg
    pg = jax.nn.softmax(logits[:, :n_groups], axis=-1)
    g_idx = jnp.argmax(pg, axis=-1)
    p_sel = jnp.take_along_axis(pg, g_idx[:, None], axis=-1)
    le = logits[:, n_groups:n_groups + n_exp].reshape(T, n_groups, epg)
    le = jnp.take_along_axis(le, g_idx[:, None, None], axis=1)[:, 0]
    pe = jax.nn.softmax(le, axis=-1)
    top_p, top_i = lax.top_k(pe, TOP_K)
    gate = p_sel * top_p / jnp.sum(top_p, axis=-1, keepdims=True)
    expert = (g_idx[:, None] * epg + top_i).astype(jnp.int32)

    A = T * TOP_K
    e_flat = expert.reshape(A)
    tok_flat = jnp.repeat(jnp.arange(T, dtype=jnp.int32), TOP_K)
    onehot = (e_flat[:, None] == jnp.arange(n_exp, dtype=jnp.int32)[None, :]).astype(jnp.int32)
    csum = jnp.cumsum(onehot, axis=0)
    counts = csum[-1]
    rank = jnp.take_along_axis(csum, e_flat[:, None], axis=1)[:, 0] - 1
    padded = (counts + tm - 1) // tm * tm
    pends = jnp.cumsum(padded)
    pstarts = pends - padded
    pos = (pstarts[e_flat] + rank).astype(jnp.int32)
    n_blocks = -(-A // tm) + n_exp
    P = n_blocks * tm
    buf_tok = jnp.zeros((P,), jnp.int32).at[pos].set(tok_flat)
    blk_e = jnp.clip(jnp.searchsorted(pends, jnp.arange(n_blocks, dtype=pends.dtype) * tm, side='right'),
                     0, n_exp - 1).astype(jnp.int32)
    nused = (pends[-1] // tm).astype(jnp.int32).reshape(1)
    pos2 = pos.reshape(T, TOP_K)
    return blk_e, nused, buf_tok, gate.astype(F32), pos2[:, 0], pos2[:, 1]


def _rope_tables(n_tokens, dk):
    freqs = dk // 4
    rows = n_tokens // GRID_W
    row = jnp.repeat(jnp.arange(rows, dtype=F32), GRID_W)
    col = jnp.tile(jnp.arange(GRID_W, dtype=F32), rows)
    inv = 1.0 / (ROPE_THETA ** (jnp.arange(freqs, dtype=F32) / freqs))
    ar, ac = row[:, None] * inv, col[:, None] * inv
    z = jnp.zeros_like(ar)
    cos = jnp.concatenate([jnp.cos(ar), jnp.cos(ar), jnp.cos(ac), jnp.cos(ac)], axis=1)
    s_lo = jnp.concatenate([-jnp.sin(ar), z, -jnp.sin(ac), z], axis=1)
    s_hi = jnp.concatenate([z, jnp.sin(ar), z, jnp.sin(ac)], axis=1)
    return cos, s_lo, s_hi


def kernel(x_prompt, x_sample, cache_k, cache_v, state_gla, c, c_ctx, w_mod, b_mod, w_in, b_gate, da_lambda,
           da_norm_w, gla_a2, gla_a_bias, gla_norm_w, w_da_proj, w_gla_proj, w_out, ln1_g, ln1_b, ln2_g, ln2_b,
           router_g_w, router_g_b, router_e_w, router_e_b, exp_w1, exp_w3, exp_w2):
    B, S, D = x_prompt.shape
    n_dec, dec_seq, _ = x_sample.shape
    depth = w_mod.shape[0]
    past, da_heads, _, dk = cache_k.shape[2:]
    dv = cache_v.shape[-1]
    gla_heads, gk, gv = state_gla.shape[3:]
    lowrank = gla_a2.shape[2]
    n_groups, epg = router_e_w.shape[2:]
    n_exp = n_groups * epg
    da_qw, da_vw = da_heads * 2 * dk, da_heads * dv
    gla_qw, gla_vw = gla_heads * gk, gla_heads * gv
    n_main = 2 * da_qw + da_vw + 2 * gla_qw + 2 * gla_vw
    assert n_main % LANE == 0 and w_in.shape[2] == n_main + 2 * lowrank + 2 * D
    alpha = (2.0 * depth) ** 0.25

    slab = _Slab(B * S, dec_seq, n_dec)
    t_ctx = slab.t_ctx
    moe_tm = 256

    x = jnp.concatenate([x_prompt.reshape(t_ctx, D), x_sample.reshape(n_dec * dec_seq, D)], axis=0)
    n_cond = 8
    cond = jnp.zeros((n_cond, D), F32).at[0].set(c_ctx).at[1:1 + n_dec].set(c)
    b_gate4 = b_gate.reshape(depth, 2, 1, D)
    nr = -(-(n_groups + n_exp) // LANE) * LANE
    wr = jnp.concatenate([router_g_w, router_e_w.reshape(depth, D, n_exp),
                          jnp.zeros((depth, D, nr - n_groups - n_exp), F32)], axis=2)
    br = jnp.concatenate([router_g_b, router_e_b.reshape(depth, n_exp),
                          jnp.zeros((depth, nr - n_groups - n_exp), F32)], axis=1).reshape(depth, 1, nr)
    a2p = jnp.zeros((depth, 2, 2 * lowrank, gla_qw), F32)
    a2p = a2p.at[:, 0, :lowrank].set(gla_a2[:, 0]).at[:, 1, lowrank:].set(gla_a2[:, 1])
    ab = gla_a_bias.reshape(depth, 2, 1, gla_qw)
    lv = da_lambda.astype(F32)
    lam_init = [0.8 - 0.6 * math.exp(-0.3 * l) for l in range(depth)]
    lam = [(jnp.exp(jnp.sum(lv[l, 0] * lv[l, 1])) - jnp.exp(jnp.sum(lv[l, 2] * lv[l, 3])) + lam_init[l]).reshape(1, 1)
           for l in range(depth)]
    rope = _rope_tables(dec_seq, dk)
    cache_k2 = cache_k.reshape(n_dec, depth, past, da_qw)
    cache_v2 = cache_v.reshape(n_dec, depth, past, da_vw)
    ln1_g3, ln1_b3 = ln1_g.reshape(depth, 1, D), ln1_b.reshape(depth, 1, D)
    ln2_g3, ln2_b3 = ln2_g.reshape(depth, 1, D), ln2_b.reshape(depth, 1, D)

    mod = _modulation(cond, w_mod, b_mod)
    mod5 = mod.reshape(depth, n_cond, 6, 1, D)

    h = _premod(x, mod5, slab, 0)
    ks, vs, sts = [], [], []
    for l in range(depth):
        u = _matmul(h, w_in, l, 0, n_main)
        ag = _matmul(h, w_in, l, n_main, LANE, tn=LANE)
        gates = _matmul(h, w_in, l, n_main + 2 * lowrank, 2 * D)
        ks.append(u[:t_ctx, da_qw:2 * da_qw].reshape(B, S, da_heads, 2, dk))
        vs.append(u[:t_ctx, 2 * da_qw:2 * da_qw + da_vw].reshape(B, S, da_heads, dv))

        post = 1.0 - lam_init[l]
        nw_d = da_norm_w[l].reshape(1, dv)
        od_c = _attn_ctx(u, lam[l], nw_d, n_seq=B, L=S, heads=da_heads, dk=dk, dv=dv, post=post)
        od_d = _attn_dec(u, t_ctx, cache_k2, cache_v2, l, rope, lam[l], nw_d,
                         n_seq=n_dec, L=dec_seq, heads=da_heads, dk=dk, dv=dv, post=post)
        od = jnp.concatenate([od_c, od_d], axis=0)

        qcol = 2 * da_qw + da_vw
        vcol = qcol + 2 * gla_qw
        rcol = vcol + gla_vw
        nw_g = gla_norm_w[l].reshape(1, gv)
        gla_kw = dict(heads=gla_heads, K=gk, V=gv, qcol=qcol, vcol=vcol, rcol=rcol)
        og_c, st = _gla(u, ag, 0, a2p[l], ab[l], nw_g, None, n_seq=B, L=S, want_state=True, **gla_kw)
        og_d, _ = _gla(u, ag, t_ctx, a2p[l], ab[l], nw_g, state_gla[:, l], n_seq=n_dec, L=dec_seq,
                       want_state=False, **gla_kw)
        og = jnp.concatenate([og_c, og_d], axis=0)
        sts.append(st)

        m = _merge(od, og, w_da_proj, w_gla_proj, gates, b_gate4, l)
        y = _matmul(m, w_out, l, 0, D)
        x1, h2, logits = _post_mixer(x, y, mod5, slab, l, ln1_g3, ln1_b3, wr, br, alpha=alpha)

        blk_e, nused, buf_tok, gate, p0, p1 = _route(logits, n_groups, epg, moe_tm)
        es = _moe(h2, blk_e, nused, buf_tok, exp_w1, exp_w3, exp_w2, l, tm=moe_tm)
        x, h = _post_moe(p0, p1, es, x1, gate, mod5, slab, l, ln2_g3, ln2_b3, alpha=alpha, has_next=l + 1 < depth)

    y_prompt = x[:t_ctx].reshape(B, S, D)
    y_sample = x[t_ctx:].reshape(n_dec, dec_seq, D)
    return (y_prompt, y_sample, jnp.stack(ks, axis=1), jnp.stack(vs, axis=1), jnp.stack(sts, axis=1))
```

```python
import functools
import math

import jax
import jax.numpy as jnp
from jax import lax
from jax.experimental import pallas as pl
from jax.experimental.pallas import tpu as pltpu

F32 = jnp.float32
BF16 = jnp.bfloat16

GRID_W = 64
ROPE_THETA = 10000.0
GLA_TAU = 16.0
GLA_CHUNK = 64
TOP_K = 2
LN_EPS = 1e-5
LANE = 128
VMEM_LIMIT = 56 * 1024 * 1024
DMA_UNROLL = 8


def _cparams(n_axes):
    return pltpu.CompilerParams(dimension_semantics=("arbitrary",) * n_axes, vmem_limit_bytes=VMEM_LIMIT)


def _tile(n, pref):
    t = min(n, pref)
    while n % t:
        t //= 2
    return t


def _silu(x):
    return x * (1.0 / (1.0 + jnp.exp(-x)))


def _sigmoid(x):
    return 1.0 / (1.0 + jnp.exp(-x))


def _mm_kernel(x_ref, w_ref, o_ref, wb_ref):
    @pl.when(pl.program_id(1) == 0)
    def _():
        wb_ref[...] = w_ref[...].astype(BF16)

    o_ref[...] = jnp.dot(x_ref[...], wb_ref[...], preferred_element_type=F32).astype(o_ref.dtype)


def _matmul(x, w, l, col0, ncols, *, tm=1024, tn=512, out_dtype=F32):
    M, K = x.shape
    tm = _tile(M, tm)
    tn = _tile(ncols, tn)
    assert col0 % tn == 0
    cb0 = col0 // tn
    return pl.pallas_call(
        _mm_kernel,
        grid=(ncols // tn, M // tm),
        in_specs=[
            pl.BlockSpec((tm, K), lambda j, i: (i, 0)),
            pl.BlockSpec((None, K, tn), lambda j, i: (l, 0, cb0 + j)),
        ],
        out_specs=pl.BlockSpec((tm, tn), lambda j, i: (i, j)),
        out_shape=jax.ShapeDtypeStruct((M, ncols), out_dtype),
        scratch_shapes=[pltpu.VMEM((K, tn), BF16)],
        compiler_params=_cparams(2),
    )(x, w)


_NT = (((1,), (1,)), ((), ()))


def _mm_t_kernel(x_ref, w_ref, o_ref, wb_ref):
    @pl.when(pl.program_id(1) == 0)
    def _():
        wb_ref[...] = w_ref[...].astype(BF16)

    o_ref[...] = lax.dot_general(x_ref[...], wb_ref[...], _NT, preferred_element_type=F32).astype(o_ref.dtype)


def _mm_t_shift_kernel(x_ref, wa_ref, wt_ref, o_ref, wb_ref, *, shift):
    @pl.when(pl.program_id(1) == 0)
    def _():
        tn = wb_ref.shape[0]
        wb_ref[0:tn - shift, :] = wa_ref[shift:, :].astype(BF16)
        wb_ref[tn - shift:, :] = wt_ref[:shift, :].astype(BF16)

    o_ref[...] = lax.dot_general(x_ref[...], wb_ref[...], _NT, preferred_element_type=F32).astype(o_ref.dtype)


def _matmul_t(x, wt, l, row0, nrows, *, tm=1024, tn=512, out_dtype=F32):
    M, K = x.shape
    tm = _tile(M, tm)
    tn = _tile(nrows, tn)
    base = row0 // LANE * LANE
    shift = row0 - base
    assert base % tn == 0 and tn % LANE == 0 and shift % 32 == 0
    rb0 = base // tn
    in_specs = [
        pl.BlockSpec((tm, K), lambda j, i: (i, 0)),
        pl.BlockSpec((None, tn, K), lambda j, i: (l, rb0 + j, 0)),
    ]
    args = [x, wt]
    kern = _mm_t_kernel
    if shift:
        lpt = tn // LANE
        in_specs.append(pl.BlockSpec((None, LANE, K), lambda j, i: (l, (rb0 + j + 1) * lpt, 0)))
        args.append(wt)
        kern = functools.partial(_mm_t_shift_kernel, shift=shift)
    return pl.pallas_call(
        kern,
        grid=(nrows // tn, M // tm),
        in_specs=in_specs,
        out_specs=pl.BlockSpec((tm, tn), lambda j, i: (i, j)),
        out_shape=jax.ShapeDtypeStruct((M, nrows), out_dtype),
        scratch_shapes=[pltpu.VMEM((tn, K), BF16)],
        compiler_params=_cparams(2),
    )(*args)


def _mod_kernel(c_ref, w_ref, b_ref, o_ref):
    a = _silu(c_ref[...]).astype(BF16)
    o_ref[...] = jnp.dot(a, w_ref[...].astype(BF16), preferred_element_type=F32) + b_ref[...]


def _modulation(cond, w_mod, b_mod, *, tn=512):
    depth, D, N = w_mod.shape
    R = cond.shape[0]
    tn = _tile(N, tn)
    return pl.pallas_call(
        _mod_kernel,
        grid=(depth, N // tn),
        in_specs=[
            pl.BlockSpec((R, D), lambda l, j: (0, 0)),
            pl.BlockSpec((None, D, tn), lambda l, j: (l, 0, j)),
            pl.BlockSpec((None, 1, tn), lambda l, j: (l, 0, j)),
        ],
        out_specs=pl.BlockSpec((None, R, tn), lambda l, j: (l, 0, j)),
        out_shape=jax.ShapeDtypeStruct((depth, R, N), F32),
        compiler_params=_cparams(2),
    )(cond, w_mod, b_mod.reshape(depth, 1, N))


class _Slab:
    def __init__(self, t_ctx, dec_seq, n_dec):
        self.t_ctx, self.dec_seq, self.n_dec = t_ctx, dec_seq, n_dec
        self.T = t_ctx + dec_seq * n_dec

    def tile(self, pref):
        t = min(pref, self.t_ctx, self.dec_seq)
        while self.t_ctx % t or self.dec_seq % t:
            t //= 2
        return t

    def mod_spec(self, tm, l, which, D):
        n_ctx = self.t_ctx // tm
        per = self.dec_seq // tm

        def imap(i, *_):
            return (l, jnp.where(i < n_ctx, 0, (i - n_ctx) // per + 1), which, 0, 0)

        return pl.BlockSpec((None, None, None, 1, D), imap)


def _premod_kernel(x_ref, sc_ref, sh_ref, h_ref):
    h_ref[...] = (x_ref[...] * (1.0 + sc_ref[...]) + sh_ref[...]).astype(h_ref.dtype)


def _premod(x, mod5, slab, l):
    T, D = x.shape
    tm = slab.tile(256)
    return pl.pallas_call(
        _premod_kernel,
        grid=(T // tm,),
        in_specs=[pl.BlockSpec((tm, D), lambda i: (i, 0)), slab.mod_spec(tm, l, 1, D), slab.mod_spec(tm, l, 0, D)],
        out_specs=pl.BlockSpec((tm, D), lambda i: (i, 0)),
        out_shape=jax.ShapeDtypeStruct((T, D), BF16),
        compiler_params=_cparams(1),
    )(x, mod5, mod5)


def _softmax_rows(s):
    m = jnp.max(s, axis=-1, keepdims=True)
    e = jnp.exp(s - m)
    return e / jnp.sum(e, axis=-1, keepdims=True)


def _diff_attn_rows(q, k1, k2, v, lam, scale, dk):
    nt = (((1,), (1,)), ((), ()))
    s1 = lax.dot_general(q[:, :dk].astype(BF16), k1, nt, preferred_element_type=F32) * scale
    s2 = lax.dot_general(q[:, dk:].astype(BF16), k2, nt, preferred_element_type=F32) * scale
    a = _softmax_rows(s1) - lam * _softmax_rows(s2)
    return jnp.dot(a.astype(BF16), v, preferred_element_type=F32)


def _rms_rows(o, w):
    return o * lax.rsqrt(jnp.mean(o * o, axis=-1, keepdims=True) + LN_EPS) * w


def _attn_ctx_kernel(lam_ref, q_ref, k_ref, v_ref, nw_ref, o_ref, *, heads, dk, dv, post):
    lam = lam_ref[0, 0]
    scale = dk ** -0.5
    for h in range(heads):
        q = q_ref[:, h * 2 * dk:(h + 1) * 2 * dk]
        k = k_ref[:, h * 2 * dk:(h + 1) * 2 * dk].astype(BF16)
        v = v_ref[:, h * dv:(h + 1) * dv].astype(BF16)
        o = _diff_attn_rows(q, k[:, :dk], k[:, dk:], v, lam, scale, dk)
        o_ref[:, h * dv:(h + 1) * dv] = (_rms_rows(o, nw_ref[...]) * post).astype(o_ref.dtype)


def _attn_ctx(u, lam, norm_w, *, n_seq, L, heads, dk, dv, post):
    qw = heads * 2 * dk
    vw = heads * dv
    assert qw == vw
    kern = functools.partial(_attn_ctx_kernel, heads=heads, dk=dk, dv=dv, post=post)
    return pl.pallas_call(
        kern,
        grid=(n_seq,),
        in_specs=[
            pl.BlockSpec(memory_space=pltpu.SMEM),
            pl.BlockSpec((L, qw), lambda b: (b, 0)),
            pl.BlockSpec((L, qw), lambda b: (b, 1)),
            pl.BlockSpec((L, vw), lambda b: (b, 2)),
            pl.BlockSpec((1, dv), lambda b: (0, 0)),
        ],
        out_specs=pl.BlockSpec((L, vw), lambda b: (b, 0)),
        out_shape=jax.ShapeDtypeStruct((n_seq * L, vw), BF16),
        compiler_params=_cparams(1),
    )(lam, u, u, u, norm_w)


def _rope_rows(x, cos, s_lo, s_hi, dk):
    q4 = dk // 4
    return x * cos + pltpu.roll(x, dk - q4, 1) * s_lo + pltpu.roll(x, q4, 1) * s_hi


def _attn_dec_kernel(lam_ref, q_ref, k_ref, v_ref, ck_ref, cv_ref, cos_ref, slo_ref, shi_ref, nw_ref, o_ref,
                     kb_ref, vb_ref, *, dk, dv, past, L, tq, post):
    lam = lam_ref[0, 0]
    scale = dk ** -0.5
    cos, slo, shi = cos_ref[...], slo_ref[...], shi_ref[...]
    kb_ref[0:past, :] = ck_ref[...].astype(BF16)
    vb_ref[0:past, :] = cv_ref[...].astype(BF16)
    for c in range(2):
        kb_ref[past:past + L, c * dk:(c + 1) * dk] = _rope_rows(
            k_ref[:, c * dk:(c + 1) * dk], cos, slo, shi, dk).astype(BF16)
    vb_ref[past:past + L, :] = v_ref[...].astype(BF16)
    k1 = kb_ref[:, :dk]
    k2 = kb_ref[:, dk:]
    v = vb_ref[...]
    for i in range(L // tq):
        rows = slice(i * tq, (i + 1) * tq)
        q = jnp.concatenate(
            [_rope_rows(q_ref[rows, c * dk:(c + 1) * dk], cos[rows], slo[rows], shi[rows], dk) for c in range(2)],
            axis=1)
        o = _diff_attn_rows(q, k1, k2, v, lam, scale, dk)
        o_ref[rows, :] = (_rms_rows(o, nw_ref[...]) * post).astype(o_ref.dtype)


def _attn_dec(u, row0, cache_k, cache_v, l, rope, lam, norm_w, *, n_seq, L, heads, dk, dv, post):
    past = cache_k.shape[2]
    assert row0 % L == 0
    rb0 = row0 // L
    tq = _tile(L, 256)
    kern = functools.partial(_attn_dec_kernel, dk=dk, dv=dv, past=past, L=L, tq=tq, post=post)
    cos, slo, shi = rope
    tab = pl.BlockSpec((L, dk), lambda b, h: (0, 0))
    return pl.pallas_call(
        kern,
        grid=(n_seq, heads),
        in_specs=[
            pl.BlockSpec(memory_space=pltpu.SMEM),
            pl.BlockSpec((L, 2 * dk), lambda b, h: (rb0 + b, h)),
            pl.BlockSpec((L, 2 * dk), lambda b, h: (rb0 + b, heads + h)),
            pl.BlockSpec((L, dv), lambda b, h: (rb0 + b, 2 * heads + h)),
            pl.BlockSpec((None, None, past, 2 * dk), lambda b, h: (b, l, 0, h)),
            pl.BlockSpec((None, None, past, dv), lambda b, h: (b, l, 0, h)),
            tab, tab, tab,
            pl.BlockSpec((1, dv), lambda b, h: (0, 0)),
        ],
        out_specs=pl.BlockSpec((L, dv), lambda b, h: (b, h)),
        out_shape=jax.ShapeDtypeStruct((n_seq * L, heads * dv), BF16),
        scratch_shapes=[pltpu.VMEM((past + L, 2 * dk), BF16), pltpu.VMEM((past + L, dv), BF16)],
        compiler_params=_cparams(2),
    )(lam, u, u, u, cache_k, cache_v, cos, slo, shi, norm_w)


def _log_sigmoid(x):
    return -(jnp.maximum(-x, 0.0) + jnp.log1p(jnp.exp(-jnp.abs(x))))


def _split3_bf16(x):
    hi = x.astype(BF16)
    r1 = x - hi.astype(F32)
    mid = r1.astype(BF16)
    lo = (r1 - mid.astype(F32)).astype(BF16)
    return hi, mid, lo


def _gla_kernel(*refs, L, K, V, G, has_s0, want_state):
    q_ref, k_ref, v_ref, r_ref, ag_ref, a2_ref, ab_ref, nw_ref = refs[:8]
    refs = refs[8:]
    s0_ref = None
    if has_s0:
        s0_ref, refs = refs[0], refs[1:]
    o_ref, refs = refs[0], refs[1:]
    sfin_ref = None
    if want_state:
        sfin_ref, refs = refs[0], refs[1:]
    st_ref, of_ref, ob_ref = refs

    C = GLA_CHUNK
    N = L // C
    r2 = a2_ref.shape[1]
    row = lax.broadcasted_iota(jnp.int32, (C, C), 0)
    col = lax.broadcasted_iota(jnp.int32, (C, C), 1)
    nt = (((1,), (1,)), ((), ()))
    tn = (((0,), (0,)), ((), ()))
    keeps = (col <= row, col >= row)
    tris = tuple(jnp.where(kp, 1.0, 0.0).astype(BF16) for kp in keeps)

    for d in range(2):
        for g in range(G):
            st_ref[d, g] = s0_ref[d, g].T if has_s0 else jnp.zeros((V, K), F32)

    def body(i, carry):
        for d in range(2):
            n = i if d == 0 else N - 1 - i
            sl = pl.ds(pl.multiple_of(n * C, C), C)
            agc = ag_ref[sl, :][:, :r2].astype(BF16)
            logit = jnp.dot(agc, a2_ref[d].astype(BF16), preferred_element_type=F32) + ab_ref[d]
            la = _log_sigmoid(logit) * (1.0 / GLA_TAU)
            hi, mid, lo = _split3_bf16(la)
            bc = (jnp.dot(tris[d], hi, preferred_element_type=F32) + jnp.dot(tris[d], mid, preferred_element_type=F32)
                  + jnp.dot(tris[d], lo, preferred_element_type=F32))
            blast = bc[C - 1:C, :] if d == 0 else bc[0:1, :]
            kc = k_ref[sl, :]
            q_in_all = (q_ref[sl, :] * (K ** -0.5) * jnp.exp(bc)).astype(BF16)
            k_in_all = (kc * jnp.exp(-bc)).astype(BF16)
            k_out_all = (kc * jnp.exp(blast - bc)).astype(BF16)
            dec_all = jnp.exp(blast)
            for g in range(G):
                ks = slice(g * K, (g + 1) * K)
                vs = slice(g * V, (g + 1) * V)
                q_in = q_in_all[:, ks]
                vc = v_ref[sl, vs].astype(BF16)
                att = lax.dot_general(q_in, k_in_all[:, ks], nt, preferred_element_type=F32)
                att = jnp.where(keeps[d], att, 0.0).astype(BF16)
                st = st_ref[d, g]
                o = jnp.dot(att, vc, preferred_element_type=F32)
                o = o + lax.dot_general(q_in, st.astype(BF16), nt, preferred_element_type=F32)
                (of_ref if d == 0 else ob_ref)[sl, vs] = o
                ds_t = lax.dot_general(vc, k_out_all[:, ks], tn, preferred_element_type=F32)
                st_ref[d, g] = dec_all[:, ks] * st + ds_t
        return carry

    lax.fori_loop(0, N, body, 0)
    if want_state:
        for d in range(2):
            for g in range(G):
                sfin_ref[d, g] = st_ref[d, g].T
    for g in range(G):
        vs = slice(g * V, (g + 1) * V)
        og = _rms_rows(of_ref[:, vs] + ob_ref[:, vs], nw_ref[...]) * _silu(r_ref[:, vs])
        o_ref[:, vs] = og.astype(o_ref.dtype)


def _gla(u, ag, row0, a2p, ab, norm_w, s0, *, n_seq, L, heads, K, V, qcol, vcol, rcol, want_state):
    G = _tile(heads, 4)
    gk, gv = G * K, G * V
    assert row0 % L == 0 and qcol % gk == 0 and (heads * K) % gk == 0 and vcol % gv == 0 and rcol % gv == 0
    rb0 = row0 // L
    qb, kb, vb, rb = qcol // gk, (qcol + heads * K) // gk, vcol // gv, rcol // gv
    r2 = a2p.shape[1]
    has_s0 = s0 is not None
    kern = functools.partial(_gla_kernel, L=L, K=K, V=V, G=G, has_s0=has_s0, want_state=want_state)
    in_specs = [
        pl.BlockSpec((L, gk), lambda b, h: (rb0 + b, qb + h)),
        pl.BlockSpec((L, gk), lambda b, h: (rb0 + b, kb + h)),
        pl.BlockSpec((L, gv), lambda b, h: (rb0 + b, vb + h)),
        pl.BlockSpec((L, gv), lambda b, h: (rb0 + b, rb + h)),
        pl.BlockSpec((L, LANE), lambda b, h: (rb0 + b, 0)),
        pl.BlockSpec((2, r2, gk), lambda b, h: (0, 0, h)),
        pl.BlockSpec((2, 1, gk), lambda b, h: (0, 0, h)),
        pl.BlockSpec((1, V), lambda b, h: (0, 0)),
    ]
    args = [u, u, u, u, ag, a2p, ab, norm_w]
    if has_s0:
        in_specs.append(pl.BlockSpec((None, 2, G, K, V), lambda b, h: (b, 0, h, 0, 0)))
        args.append(s0)
    out_specs = [pl.BlockSpec((L, gv), lambda b, h: (b, h))]
    out_shape = [jax.ShapeDtypeStruct((n_seq * L, heads * V), BF16)]
    if want_state:
        out_specs.append(pl.BlockSpec((None, 2, G, K, V), lambda b, h: (b, 0, h, 0, 0)))
        out_shape.append(jax.ShapeDtypeStruct((n_seq, 2, heads, K, V), F32))
    res = pl.pallas_call(
        kern,
        grid=(n_seq, heads // G),
        in_specs=in_specs,
        out_specs=out_specs,
        out_shape=out_shape,
        scratch_shapes=[pltpu.VMEM((2, G, V, K), F32), pltpu.VMEM((L, gv), F32), pltpu.VMEM((L, gv), F32)],
        compiler_params=_cparams(2),
    )(*args)
    return res if want_state else (res[0], None)


def _merge_kernel(odc_ref, odd_ref, ogc_ref, ogd_ref, wd_ref, wg_ref, gd_ref, gg_ref, bd_ref, bg_ref, o_ref,
                  wdb_ref, wgb_ref, *, n_ctx):
    i = pl.program_id(1)

    @pl.when(i == 0)
    def _():
        wdb_ref[...] = wd_ref[...].astype(BF16)
        wgb_ref[...] = wg_ref[...].astype(BF16)

    def run(od_ref, og_ref):
        yd = jnp.dot(od_ref[...], wdb_ref[...], preferred_element_type=F32)
        yg = jnp.dot(og_ref[...], wgb_ref[...], preferred_element_type=F32)
        m = _sigmoid(gd_ref[...] + bd_ref[...]) * yd + _sigmoid(gg_ref[...] + bg_ref[...]) * yg
        o_ref[...] = m.astype(o_ref.dtype)

    @pl.when(i < n_ctx)
    def _():
        run(odc_ref, ogc_ref)

    @pl.when(i >= n_ctx)
    def _():
        run(odd_ref, ogd_ref)


def _merge(od_c, od_d, og_c, og_d, w_da, w_gla, gates, b_gate4, l, *, tm=512, tn=512):
    t_ctx, Kd = od_c.shape
    t_dec = od_d.shape[0]
    Kg = og_c.shape[1]
    D = w_da.shape[2]
    tm = _tile(math.gcd(t_ctx, t_dec), tm)
    tn = _tile(D, tn)
    nj = D // tn
    n_ctx = t_ctx // tm
    T = t_ctx + t_dec
    ctx_rows = lambda j, i: (jnp.minimum(i, n_ctx - 1), 0)
    dec_rows = lambda j, i: (jnp.maximum(i - n_ctx, 0), 0)
    return pl.pallas_call(
        functools.partial(_merge_kernel, n_ctx=n_ctx),
        grid=(nj, T // tm),
        in_specs=[
            pl.BlockSpec((tm, Kd), ctx_rows),
            pl.BlockSpec((tm, Kd), dec_rows),
            pl.BlockSpec((tm, Kg), ctx_rows),
            pl.BlockSpec((tm, Kg), dec_rows),
            pl.BlockSpec((None, Kd, tn), lambda j, i: (l, 0, j)),
            pl.BlockSpec((None, Kg, tn), lambda j, i: (l, 0, j)),
            pl.BlockSpec((tm, tn), lambda j, i: (i, j)),
            pl.BlockSpec((tm, tn), lambda j, i: (i, nj + j)),
            pl.BlockSpec((None, None, 1, tn), lambda j, i: (l, 0, 0, j)),
            pl.BlockSpec((None, None, 1, tn), lambda j, i: (l, 1, 0, j)),
        ],
        out_specs=pl.BlockSpec((tm, tn), lambda j, i: (i, j)),
        out_shape=jax.ShapeDtypeStruct((T, D), BF16),
        scratch_shapes=[pltpu.VMEM((Kd, tn), BF16), pltpu.VMEM((Kg, tn), BF16)],
        compiler_params=_cparams(2),
    )(od_c, od_d, og_c, og_d, w_da, w_gla, gates, gates, b_gate4, b_gate4)


def _layer_norm_rows(z, g, b):
    mu = jnp.mean(z, axis=-1, keepdims=True)
    zc = z - mu
    var = jnp.mean(zc * zc, axis=-1, keepdims=True)
    return zc * lax.rsqrt(var + LN_EPS) * g + b


def _post_mixer_kernel(x_ref, y_ref, g1_ref, sc_ref, sh_ref, lg_ref, lb_ref, wr_ref, br_ref,
                       x1_ref, h2_ref, lo_ref, *, alpha):
    x1 = _layer_norm_rows(alpha * x_ref[...] + g1_ref[...] * y_ref[...], lg_ref[...], lb_ref[...])
    x1_ref[...] = x1
    h2 = x1 * (1.0 + sc_ref[...]) + sh_ref[...]
    h2_ref[...] = h2.reshape(h2_ref.shape)
    lo_ref[...] = jnp.dot(h2, wr_ref[...], preferred_element_type=F32,
                          precision=lax.Precision.HIGHEST) + br_ref[...]


def _post_mixer(x, y, mod5, slab, l, ln_g, ln_b, wr, br, *, alpha):
    T, D = x.shape
    tm = slab.tile(256)
    nr = wr.shape[2]
    row = pl.BlockSpec((tm, D), lambda i: (i, 0))
    vec = pl.BlockSpec((None, 1, D), lambda i: (l, 0, 0))
    return pl.pallas_call(
        functools.partial(_post_mixer_kernel, alpha=alpha),
        grid=(T // tm,),
        in_specs=[row, row, slab.mod_spec(tm, l, 2, D), slab.mod_spec(tm, l, 4, D), slab.mod_spec(tm, l, 3, D),
                  vec, vec,
                  pl.BlockSpec((None, D, nr), lambda i: (l, 0, 0)),
                  pl.BlockSpec((None, 1, nr), lambda i: (l, 0, 0))],
        out_specs=[row, pl.BlockSpec((tm, 1, D), lambda i: (i, 0, 0)), pl.BlockSpec((tm, nr), lambda i: (i, 0))],
        out_shape=[jax.ShapeDtypeStruct((T, D), F32), jax.ShapeDtypeStruct((T, 1, D), F32),
                   jax.ShapeDtypeStruct((T, nr), F32)],
        compiler_params=_cparams(1),
    )(x, y, mod5, mod5, mod5, ln_g, ln_b, wr, br)


def _issue_rows(copy_fn, tm):
    def f(r, c):
        for cp in copy_fn(r):
            cp.start()
        return c
    lax.fori_loop(0, tm, f, 0, unroll=DMA_UNROLL)


def _wait_rows(src_hbm, dst, sem):
    pltpu.make_async_copy(src_hbm.at[pl.ds(0, dst.shape[0])], dst, sem).wait()


def _moe_gather_kernel(nused_ref, tok_ref, x_hbm, o_ref, buf_ref, dense_ref, sem, *, tm):
    b = pl.program_id(0)
    nused = nused_ref[0]

    def issue(blk, slot):
        def copy_fn(r):
            t = tok_ref[blk * tm + r]
            return (pltpu.make_async_copy(x_hbm.at[pl.ds(t, 1)], buf_ref.at[slot, pl.ds(r, 1)], sem.at[slot]),)
        _issue_rows(copy_fn, tm)

    @pl.when(jnp.logical_and(b == 0, nused > 0))
    def _():
        issue(0, 0)

    @pl.when(b + 1 < nused)
    def _():
        issue(b + 1, (b + 1) % 2)

    @pl.when(b < nused)
    def _():
        _wait_rows(x_hbm, buf_ref.at[b % 2], sem.at[b % 2])
        dense_ref[...] = buf_ref[b % 2].reshape(dense_ref.shape)
        o_ref[...] = dense_ref[...].astype(o_ref.dtype)

    @pl.when(b >= nused)
    def _():
        o_ref[...] = jnp.zeros_like(o_ref)


def _moe_gather(h2, nused, buf_tok, *, tm):
    D = h2.shape[2]
    P = buf_tok.shape[0]
    grid_spec = pltpu.PrefetchScalarGridSpec(
        num_scalar_prefetch=2,
        grid=(P // tm,),
        in_specs=[pl.BlockSpec(memory_space=pl.ANY)],
        out_specs=pl.BlockSpec((tm, D), lambda b, nu, tk: (b, 0)),
        scratch_shapes=[pltpu.VMEM((2, tm, 1, D), F32), pltpu.VMEM((tm, D), F32), pltpu.SemaphoreType.DMA((2,))],
    )
    return pl.pallas_call(
        functools.partial(_moe_gather_kernel, tm=tm),
        grid_spec=grid_spec,
        out_shape=jax.ShapeDtypeStruct((P, D), BF16),
        compiler_params=_cparams(1),
    )(nused, buf_tok, h2)


def _expert_changed(blk_e_ref, b):
    return jnp.logical_or(b == 0, blk_e_ref[b] != blk_e_ref[jnp.maximum(b - 1, 0)])


def _moe_ffn_kernel(blk_e_ref, nxt_e_ref, nused_ref, xs_ref, w1_hbm, w3_hbm, w2_hbm, o_ref,
                    s1_ref, s3_ref, s2_ref, b1_ref, b3_ref, b2_ref, sem, *, l):
    b = pl.program_id(0)
    used = b < nused_ref[0]
    streams = ((w1_hbm, s1_ref, b1_ref), (w3_hbm, s3_ref, b3_ref), (w2_hbm, s2_ref, b2_ref))

    def fetch(k, e):
        w_hbm, stage_ref, _ = streams[k]
        return pltpu.make_async_copy(w_hbm.at[l, e], stage_ref, sem.at[k])

    @pl.when(jnp.logical_and(used, _expert_changed(blk_e_ref, b)))
    def _():
        e = blk_e_ref[b]
        nxt = nxt_e_ref[b]

        @pl.when(b == 0)
        def _():
            for k in range(3):
                fetch(k, e).start()

        for k in range(3):
            _, stage_ref, cast_ref = streams[k]
            fetch(k, e).wait()
            cast_ref[...] = stage_ref[...].astype(BF16)

            @pl.when(nxt >= 0)
            def _(k=k):
                fetch(k, nxt).start()

    @pl.when(used)
    def _():
        xs = xs_ref[...]
        h1 = jnp.dot(xs, b1_ref[...], preferred_element_type=F32)
        h3 = jnp.dot(xs, b3_ref[...], preferred_element_type=F32)
        hid = (_silu(h1) * h3).astype(BF16)
        o_ref[...] = jnp.dot(hid, b2_ref[...], preferred_element_type=F32)

    @pl.when(jnp.logical_not(used))
    def _():
        o_ref[...] = jnp.zeros_like(o_ref)


def _moe(h2, blk_e, nxt_e, nused, buf_tok, w1, w3, w2, l, *, tm):
    D, FF = w1.shape[2:]
    P = buf_tok.shape[0]
    xs = _moe_gather(h2, nused, buf_tok, tm=tm)
    grid_spec = pltpu.PrefetchScalarGridSpec(
        num_scalar_prefetch=3,
        grid=(P // tm,),
        in_specs=[
            pl.BlockSpec((tm, D), lambda b, be, nx, nu: (b, 0)),
            pl.BlockSpec(memory_space=pl.ANY),
            pl.BlockSpec(memory_space=pl.ANY),
            pl.BlockSpec(memory_space=pl.ANY),
        ],
        out_specs=pl.BlockSpec((tm, D), lambda b, be, nx, nu: (b, 0)),
        scratch_shapes=[pltpu.VMEM((D, FF), F32), pltpu.VMEM((D, FF), F32), pltpu.VMEM((FF, D), F32),
                        pltpu.VMEM((D, FF), BF16), pltpu.VMEM((D, FF), BF16), pltpu.VMEM((FF, D), BF16),
                        pltpu.SemaphoreType.DMA((3,))],
    )
    return pl.pallas_call(
        functools.partial(_moe_ffn_kernel, l=l),
        grid_spec=grid_spec,
        out_shape=jax.ShapeDtypeStruct((P, D), F32),
        compiler_params=_cparams(1),
    )(blk_e, nxt_e, nused, xs, w1, w3, w2)


def _post_moe_kernel(p0_ref, p1_ref, es_hbm, x_ref, gw_ref, g2_ref, lg_ref, lb_ref, *rest, tm, alpha, has_next):
    if has_next:
        sc_ref, sh_ref, x2_ref, h_ref, buf_ref, sem = rest
    else:
        x2_ref, buf_ref, sem = rest
    i = pl.program_id(0)
    n = pl.num_programs(0)
    p_refs = (p0_ref, p1_ref)

    def issue(tile, slot):
        def copy_fn(r):
            return tuple(pltpu.make_async_copy(es_hbm.at[pl.ds(p_refs[j][tile * tm + r], 1)],
                                               buf_ref.at[slot, j, pl.ds(r, 1)], sem.at[slot, j])
                         for j in range(TOP_K))
        _issue_rows(copy_fn, tm)

    @pl.when(i == 0)
    def _():
        issue(0, 0)

    @pl.when(i + 1 < n)
    def _():
        issue(i + 1, (i + 1) % 2)

    slot = i % 2
    for j in range(TOP_K):
        _wait_rows(es_hbm, buf_ref.at[slot, j], sem.at[slot, j])
    y = gw_ref[:, 0:1] * buf_ref[slot, 0] + gw_ref[:, 1:2] * buf_ref[slot, 1]
    x2 =_layer_norm_rows(alpha * x_ref[...] + g2_ref[...] * y, lg_ref[...], lb_ref[...])
    x2_ref[...] = x2
    if has_next:
        h_ref[...] = (x2 * (1.0 + sc_ref[...]) + sh_ref[...]).astype(h_ref.dtype)


def _post_moe(p0, p1, es, x1, gate, mod5, slab, l, ln_g, ln_b, *, alpha, has_next):
    T, D = x1.shape
    tm = slab.tile(256)
    row = pl.BlockSpec((tm, D), lambda i, *_: (i, 0))
    vec = pl.BlockSpec((None, 1, D), lambda i, *_: (l, 0, 0))
    in_specs = [pl.BlockSpec(memory_space=pl.ANY), row, pl.BlockSpec((tm, TOP_K), lambda i, *_: (i, 0)),
                slab.mod_spec(tm, l, 5, D), vec, vec]
    args = [es, x1, gate, mod5, ln_g, ln_b]
    out_specs = [row]
    out_shape = [jax.ShapeDtypeStruct((T, D), F32)]
    if has_next:
        in_specs += [slab.mod_spec(tm, l + 1, 1, D), slab.mod_spec(tm, l + 1, 0, D)]
        args += [mod5, mod5]
        out_specs.append(row)
        out_shape.append(jax.ShapeDtypeStruct((T, D), BF16))
    grid_spec = pltpu.PrefetchScalarGridSpec(
        num_scalar_prefetch=2,
        grid=(T // tm,),
        in_specs=in_specs,
        out_specs=out_specs,
        scratch_shapes=[pltpu.VMEM((2, TOP_K, tm, D), F32), pltpu.SemaphoreType.DMA((2, TOP_K))],
    )
    res = pl.pallas_call(
        functools.partial(_post_moe_kernel, tm=tm, alpha=alpha, has_next=has_next),
        grid_spec=grid_spec,
        out_shape=out_shape,
        compiler_params=_cparams(1),
    )(p0, p1, *args)
    return (res[0], res[1]) if has_next else (res[0], None)


def _route(logits, n_groups, epg, tm):
    T = logits.shape[0]
    n_exp = n_groups * epg
    pg = jax.nn.softmax(logits[:, :n_groups], axis=-1)
    g_idx = jnp.argmax(pg, axis=-1)
    p_sel = jnp.take_along_axis(pg, g_idx[:, None], axis=-1)
    le = logits[:, n_groups:n_groups + n_exp].reshape(T, n_groups, epg)
    le = jnp.take_along_axis(le, g_idx[:, None, None], axis=1)[:, 0]
    pe = jax.nn.softmax(le, axis=-1)
    top_p, top_i = lax.top_k(pe, TOP_K)
    gate = p_sel * top_p / jnp.sum(top_p, axis=-1, keepdims=True)
    expert = (g_idx[:, None] * epg + top_i).astype(jnp.int32)

    A = T * TOP_K
    e_flat = expert.reshape(A)
    tok_flat = jnp.repeat(jnp.arange(T, dtype=jnp.int32), TOP_K)
    onehot = (e_flat[:, None] == jnp.arange(n_exp, dtype=jnp.int32)[None, :]).astype(jnp.int32)
    csum = jnp.cumsum(onehot, axis=0)
    counts = csum[-1]
    rank = jnp.take_along_axis(csum, e_flat[:, None], axis=1)[:, 0] - 1
    padded = (counts + tm - 1) // tm * tm
    pends = jnp.cumsum(padded)
    pstarts = pends - padded
    pos = (pstarts[e_flat] + rank).astype(jnp.int32)
    n_blocks = -(-A // tm) + n_exp
    P = n_blocks * tm
    buf_tok = jnp.zeros((P,), jnp.int32).at[pos].set(tok_flat)
    blk_e = jnp.clip(jnp.searchsorted(pends, jnp.arange(n_blocks, dtype=pends.dtype) * tm, side='right'),
                     0, n_exp - 1).astype(jnp.int32)
    nused = (pends[-1] // tm).astype(jnp.int32).reshape(1)
    ids = jnp.arange(n_exp, dtype=jnp.int32)
    own = jnp.where(counts > 0, ids, n_exp)
    later = lax.cummin(jnp.concatenate([own[1:], jnp.full((1,), n_exp, jnp.int32)]), reverse=True)
    nxt_e = jnp.where(later < n_exp, later, -1)[blk_e].astype(jnp.int32)
    pos2 = pos.reshape(T, TOP_K)
    return blk_e, nxt_e, nused, buf_tok, gate.astype(F32), pos2[:, 0], pos2[:, 1]


def _rope_tables(n_tokens, dk):
    freqs = dk // 4
    rows = n_tokens // GRID_W
    row = jnp.repeat(jnp.arange(rows, dtype=F32), GRID_W)
    col = jnp.tile(jnp.arange(GRID_W, dtype=F32), rows)
    inv = 1.0 / (ROPE_THETA ** (jnp.arange(freqs, dtype=F32) / freqs))
    ar, ac = row[:, None] * inv, col[:, None] * inv
    z = jnp.zeros_like(ar)
    cos = jnp.concatenate([jnp.cos(ar), jnp.cos(ar), jnp.cos(ac), jnp.cos(ac)], axis=1)
    s_lo = jnp.concatenate([-jnp.sin(ar), z, -jnp.sin(ac), z], axis=1)
    s_hi = jnp.concatenate([z, jnp.sin(ar), z, jnp.sin(ac)], axis=1)
    return cos, s_lo, s_hi


def kernel(x_prompt, x_sample, cache_k, cache_v, state_gla, c, c_ctx, w_mod, b_mod, w_in, b_gate, da_lambda,
           da_norm_w, gla_a2, gla_a_bias, gla_norm_w, w_da_proj, w_gla_proj, w_out, ln1_g, ln1_b, ln2_g, ln2_b,
           router_g_w, router_g_b, router_e_w, router_e_b, exp_w1, exp_w3, exp_w2):
    B, S, D = x_prompt.shape
    n_dec, dec_seq, _ = x_sample.shape
    depth = w_mod.shape[0]
    past, da_heads, _, dk = cache_k.shape[2:]
    dv = cache_v.shape[-1]
    gla_heads, gk, gv = state_gla.shape[3:]
    lowrank = gla_a2.shape[2]
    n_groups, epg = router_e_w.shape[2:]
    n_exp = n_groups * epg
    da_qw, da_vw = da_heads * 2 * dk, da_heads * dv
    gla_qw, gla_vw = gla_heads * gk, gla_heads * gv
    n_main = 2 * da_qw + da_vw + 2 * gla_qw + 2 * gla_vw
    assert n_main % LANE == 0 and w_in.shape[2] == n_main + 2 * lowrank + 2 * D
    alpha = (2.0 * depth) ** 0.25

    slab = _Slab(B * S, dec_seq, n_dec)
    t_ctx = slab.t_ctx
    moe_tm = 256

    x = jnp.concatenate([x_prompt.reshape(t_ctx, D), x_sample.reshape(n_dec * dec_seq, D)], axis=0)
    n_cond = 8
    cond = jnp.zeros((n_cond, D), F32).at[0].set(c_ctx).at[1:1 + n_dec].set(c)
    w_in_t = jnp.swapaxes(w_in, 1, 2)
    b_gate4 = b_gate.reshape(depth, 2, 1, D)
    nr =-(-(n_groups + n_exp) // LANE) * LANE
    wr = jnp.concatenate([router_g_w, router_e_w.reshape(depth, D, n_exp),
                          jnp.zeros((depth, D, nr - n_groups - n_exp), F32)], axis=2)
    br = jnp.concatenate([router_g_b, router_e_b.reshape(depth, n_exp),
                          jnp.zeros((depth, nr - n_groups - n_exp), F32)], axis=1).reshape(depth, 1, nr)
    a2p = jnp.zeros((depth, 2, 2 * lowrank, gla_qw), F32)
    a2p = a2p.at[:, 0, :lowrank].set(gla_a2[:, 0]).at[:, 1, lowrank:].set(gla_a2[:, 1])
    ab = gla_a_bias.reshape(depth, 2, 1, gla_qw)
    lv = da_lambda.astype(F32)
    lam_init = [0.8 - 0.6 * math.exp(-0.3 * l) for l in range(depth)]
    lam = [(jnp.exp(jnp.sum(lv[l, 0] * lv[l, 1])) - jnp.exp(jnp.sum(lv[l, 2] * lv[l, 3])) + lam_init[l]).reshape(1, 1)
           for l in range(depth)]
    rope = _rope_tables(dec_seq, dk)
    cache_k2 = cache_k.reshape(n_dec, depth, past, da_qw)
    cache_v2 = cache_v.reshape(n_dec, depth, past, da_vw)
    ln1_g3, ln1_b3 = ln1_g.reshape(depth, 1, D), ln1_b.reshape(depth, 1, D)
    ln2_g3, ln2_b3 = ln2_g.reshape(depth, 1, D), ln2_b.reshape(depth, 1, D)

    mod = _modulation(cond, w_mod, b_mod)
    mod5 = mod.reshape(depth, n_cond, 6, 1, D)

    h = _premod(x, mod5, slab, 0)
    ks, vs, sts = [], [], []
    for l in range(depth):
        u = _matmul_t(h, w_in_t, l, 0, n_main)
        ag = _matmul_t(h, w_in_t, l, n_main, LANE, tn=LANE)
        gates = _matmul_t(h, w_in_t, l, n_main + 2 * lowrank, 2 * D)
        ks.append(u[:t_ctx, da_qw:2 * da_qw].reshape(B, S, da_heads, 2, dk))
        vs.append(u[:t_ctx, 2 * da_qw:2 * da_qw + da_vw].reshape(B, S, da_heads, dv))

        post = 1.0 - lam_init[l]
        nw_d = da_norm_w[l].reshape(1, dv)
        od_c = _attn_ctx(u, lam[l], nw_d, n_seq=B, L=S, heads=da_heads, dk=dk, dv=dv, post=post)
        od_d = _attn_dec(u, t_ctx, cache_k2, cache_v2, l, rope, lam[l], nw_d,
                         n_seq=n_dec, L=dec_seq, heads=da_heads, dk=dk, dv=dv, post=post)

        qcol = 2 * da_qw + da_vw
        vcol = qcol + 2 * gla_qw
        rcol = vcol + gla_vw
        nw_g = gla_norm_w[l].reshape(1, gv)
        gla_kw = dict(heads=gla_heads, K=gk, V=gv, qcol=qcol, vcol=vcol, rcol=rcol)
        og_c, st = _gla(u, ag, 0, a2p[l], ab[l], nw_g, None, n_seq=B, L=S, want_state=True, **gla_kw)
        og_d, _ = _gla(u, ag, t_ctx, a2p[l], ab[l], nw_g, state_gla[:, l], n_seq=n_dec, L=dec_seq,
                       want_state=False, **gla_kw)
        sts.append(st)

        m = _merge(od_c, od_d, og_c, og_d, w_da_proj, w_gla_proj, gates, b_gate4, l)
        y = _matmul(m, w_out, l, 0, D)
        x1, h2, logits = _post_mixer(x, y, mod5, slab, l, ln1_g3, ln1_b3, wr, br, alpha=alpha)

        blk_e, nxt_e, nused, buf_tok, gate, p0, p1 = _route(logits, n_groups, epg, moe_tm)
        es = _moe(h2, blk_e, nxt_e, nused, buf_tok, exp_w1, exp_w3, exp_w2, l, tm=moe_tm)
        x, h = _post_moe(p0, p1, es, x1, gate, mod5, slab, l, ln2_g3, ln2_b3, alpha=alpha, has_next=l + 1 < depth)

    y_prompt = x[:t_ctx].reshape(B, S, D)
    y_sample = x[t_ctx:].reshape(n_dec, dec_seq, D)
    return (y_prompt, y_sample, jnp.stack(ks, axis=1), jnp.stack(vs, axis=1), jnp.stack(sts, axis=1))
```

```python
import functools
import math

import jax
import jax.numpy as jnp
from jax import lax
from jax.experimental import pallas as pl
from jax.experimental.pallas import tpu as pltpu

F32 = jnp.float32
BF16 = jnp.bfloat16

GRID_W = 64
ROPE_THETA = 10000.0
GLA_TAU = 16.0
GLA_CHUNK = 64
TOP_K = 2
LN_EPS = 1e-5
LANE = 128
VMEM_LIMIT = 56 * 1024 * 1024
DMA_UNROLL = 8
ROW_SEMS = 4


def _cparams(n_axes):
    return pltpu.CompilerParams(dimension_semantics=("arbitrary",) * n_axes, vmem_limit_bytes=VMEM_LIMIT)


def _tile(n, pref):
    t = min(n, pref)
    while n % t:
        t //= 2
    return t


def _silu(x):
    return x * (1.0 / (1.0 + jnp.exp(-x)))


def _sigmoid(x):
    return 1.0 / (1.0 + jnp.exp(-x))


def _mm_kernel(x_ref, w_ref, o_ref, wb_ref):
    @pl.when(pl.program_id(1) == 0)
    def _():
        wb_ref[...] = w_ref[...].astype(BF16)

    o_ref[...] = jnp.dot(x_ref[...], wb_ref[...], preferred_element_type=F32).astype(o_ref.dtype)


def _matmul(x, w, l, col0, ncols, *, tm=1024, tn=512, out_dtype=F32):
    M, K = x.shape
    tm = _tile(M, tm)
    tn = _tile(ncols, tn)
    assert col0 % tn == 0
    cb0 = col0 // tn
    return pl.pallas_call(
        _mm_kernel,
        grid=(ncols // tn, M // tm),
        in_specs=[
            pl.BlockSpec((tm, K), lambda j, i: (i, 0)),
            pl.BlockSpec((None, K, tn), lambda j, i: (l, 0, cb0 + j)),
        ],
        out_specs=pl.BlockSpec((tm, tn), lambda j, i: (i, j)),
        out_shape=jax.ShapeDtypeStruct((M, ncols), out_dtype),
        scratch_shapes=[pltpu.VMEM((K, tn), BF16)],
        compiler_params=_cparams(2),
    )(x, w)


_NT = (((1,), (1,)), ((), ()))


def _mm_t_kernel(x_ref, w_ref, o_ref, wb_ref):
    @pl.when(pl.program_id(1) == 0)
    def _():
        wb_ref[...] = w_ref[...].astype(BF16)

    o_ref[...] = lax.dot_general(x_ref[...], wb_ref[...], _NT, preferred_element_type=F32).astype(o_ref.dtype)


def _mm_t_shift_kernel(x_ref, wa_ref, wt_ref, o_ref, wb_ref, *, shift):
    @pl.when(pl.program_id(1) == 0)
    def _():
        tn = wb_ref.shape[0]
        wb_ref[0:tn - shift, :] = wa_ref[shift:, :].astype(BF16)
        wb_ref[tn - shift:, :] = wt_ref[:shift, :].astype(BF16)

    o_ref[...] = lax.dot_general(x_ref[...], wb_ref[...], _NT, preferred_element_type=F32).astype(o_ref.dtype)


def _matmul_t(x, wt, l, row0, nrows, *, tm=1024, tn=512, out_dtype=F32):
    M, K = x.shape
    tm = _tile(M, tm)
    tn = _tile(nrows, tn)
    base = row0 // LANE * LANE
    shift = row0 - base
    assert base % tn == 0 and tn % LANE == 0 and shift % 32 == 0
    rb0 = base // tn
    in_specs = [
        pl.BlockSpec((tm, K), lambda j, i: (i, 0)),
        pl.BlockSpec((None, tn, K), lambda j, i: (l, rb0 + j, 0)),
    ]
    args = [x, wt]
    kern = _mm_t_kernel
    if shift:
        lpt = tn // LANE
        in_specs.append(pl.BlockSpec((None, LANE, K), lambda j, i: (l, (rb0 + j + 1) * lpt, 0)))
        args.append(wt)
        kern = functools.partial(_mm_t_shift_kernel, shift=shift)
    return pl.pallas_call(
        kern,
        grid=(nrows // tn, M // tm),
        in_specs=in_specs,
        out_specs=pl.BlockSpec((tm, tn), lambda j, i: (i, j)),
        out_shape=jax.ShapeDtypeStruct((M, nrows), out_dtype),
        scratch_shapes=[pltpu.VMEM((tn, K), BF16)],
        compiler_params=_cparams(2),
    )(*args)


def _mod_kernel(c_ref, w_ref, b_ref, o_ref):
    a = _silu(c_ref[...]).astype(BF16)
    o_ref[...] = jnp.dot(a, w_ref[...].astype(BF16), preferred_element_type=F32) + b_ref[...]


def _modulation(cond, w_mod, b_mod, *, tn=512):
    depth, D, N = w_mod.shape
    R = cond.shape[0]
    tn = _tile(N, tn)
    return pl.pallas_call(
        _mod_kernel,
        grid=(depth, N // tn),
        in_specs=[
            pl.BlockSpec((R, D), lambda l, j: (0, 0)),
            pl.BlockSpec((None, D, tn), lambda l, j: (l, 0, j)),
            pl.BlockSpec((None, 1, tn), lambda l, j: (l, 0, j)),
        ],
        out_specs=pl.BlockSpec((None, R, tn), lambda l, j: (l, 0, j)),
        out_shape=jax.ShapeDtypeStruct((depth, R, N), F32),
        compiler_params=_cparams(2),
    )(cond, w_mod, b_mod.reshape(depth, 1, N))


class _Slab:
    def __init__(self, t_ctx, dec_seq, n_dec):
        self.t_ctx, self.dec_seq, self.n_dec = t_ctx, dec_seq, n_dec
        self.T = t_ctx + dec_seq * n_dec

    def tile(self, pref):
        t = min(pref, self.t_ctx, self.dec_seq)
        while self.t_ctx % t or self.dec_seq % t:
            t //= 2
        return t

    def mod_spec(self, tm, l, which, D):
        n_ctx = self.t_ctx // tm
        per = self.dec_seq // tm

        def imap(i, *_):
            return (l, jnp.where(i < n_ctx, 0, (i - n_ctx) // per + 1), which, 0, 0)

        return pl.BlockSpec((None, None, None, 1, D), imap)


def _premod_kernel(x_ref, sc_ref, sh_ref, h_ref):
    h_ref[...] = (x_ref[...] * (1.0 + sc_ref[...]) + sh_ref[...]).astype(h_ref.dtype)


def _premod(x, mod5, slab, l):
    T, D = x.shape
    tm = slab.tile(256)
    return pl.pallas_call(
        _premod_kernel,
        grid=(T // tm,),
        in_specs=[pl.BlockSpec((tm, D), lambda i: (i, 0)), slab.mod_spec(tm, l, 1, D), slab.mod_spec(tm, l, 0, D)],
        out_specs=pl.BlockSpec((tm, D), lambda i: (i, 0)),
        out_shape=jax.ShapeDtypeStruct((T, D), BF16),
        compiler_params=_cparams(1),
    )(x, mod5, mod5)


def _softmax_rows(s):
    m = jnp.max(s, axis=-1, keepdims=True)
    e = jnp.exp(s - m)
    return e / jnp.sum(e, axis=-1, keepdims=True)


def _diff_attn_rows(q, k1, k2, v, lam, scale, dk):
    nt = (((1,), (1,)), ((), ()))
    s1 = lax.dot_general(q[:, :dk].astype(BF16), k1, nt, preferred_element_type=F32) * scale
    s2 = lax.dot_general(q[:, dk:].astype(BF16), k2, nt, preferred_element_type=F32) * scale
    a = _softmax_rows(s1) - lam * _softmax_rows(s2)
    return jnp.dot(a.astype(BF16), v, preferred_element_type=F32)


def _rms_rows(o, w):
    return o * lax.rsqrt(jnp.mean(o * o, axis=-1, keepdims=True) + LN_EPS) * w


def _attn_ctx_kernel(lam_ref, q_ref, k_ref, v_ref, nw_ref, o_ref, *, heads, dk, dv, post):
    lam = lam_ref[0, 0]
    scale = dk ** -0.5
    for h in range(heads):
        q = q_ref[:, h * 2 * dk:(h + 1) * 2 * dk]
        k = k_ref[:, h * 2 * dk:(h + 1) * 2 * dk].astype(BF16)
        v = v_ref[:, h * dv:(h + 1) * dv].astype(BF16)
        o = _diff_attn_rows(q, k[:, :dk], k[:, dk:], v, lam, scale, dk)
        o_ref[:, h * dv:(h + 1) * dv] = (_rms_rows(o, nw_ref[...]) * post).astype(o_ref.dtype)


def _attn_ctx(u, lam, norm_w, *, n_seq, L, heads, dk, dv, post):
    qw = heads * 2 * dk
    vw = heads * dv
    assert qw == vw
    kern = functools.partial(_attn_ctx_kernel, heads=heads, dk=dk, dv=dv, post=post)
    return pl.pallas_call(
        kern,
        grid=(n_seq,),
        in_specs=[
            pl.BlockSpec(memory_space=pltpu.SMEM),
            pl.BlockSpec((L, qw), lambda b: (b, 0)),
            pl.BlockSpec((L, qw), lambda b: (b, 1)),
            pl.BlockSpec((L, vw), lambda b: (b, 2)),
            pl.BlockSpec((1, dv), lambda b: (0, 0)),
        ],
        out_specs=pl.BlockSpec((L, vw), lambda b: (b, 0)),
        out_shape=jax.ShapeDtypeStruct((n_seq * L, vw), BF16),
        compiler_params=_cparams(1),
    )(lam, u, u, u, norm_w)


def _rope_rows(x, cos, s_lo, s_hi, dk):
    q4 = dk // 4
    return x * cos + pltpu.roll(x, dk - q4, 1) * s_lo + pltpu.roll(x, q4, 1) * s_hi


def _attn_dec_kernel(lam_ref, q_ref, k_ref, v_ref, ck_ref, cv_ref, cos_ref, slo_ref, shi_ref, nw_ref, o_ref,
                     kb_ref, vb_ref, *, dk, dv, past, L, tq, post):
    lam = lam_ref[0, 0]
    scale = dk ** -0.5
    cos, slo, shi = cos_ref[...], slo_ref[...], shi_ref[...]
    kb_ref[0:past, :] = ck_ref[...].astype(BF16)
    vb_ref[0:past, :] = cv_ref[...].astype(BF16)
    for c in range(2):
        kb_ref[past:past + L, c * dk:(c + 1) * dk] = _rope_rows(
            k_ref[:, c * dk:(c + 1) * dk], cos, slo, shi, dk).astype(BF16)
    vb_ref[past:past + L, :] = v_ref[...].astype(BF16)
    k1 = kb_ref[:, :dk]
    k2 = kb_ref[:, dk:]
    v = vb_ref[...]
    for i in range(L // tq):
        rows = slice(i * tq, (i + 1) * tq)
        q = jnp.concatenate(
            [_rope_rows(q_ref[rows, c * dk:(c + 1) * dk], cos[rows], slo[rows], shi[rows], dk) for c in range(2)],
            axis=1)
        o = _diff_attn_rows(q, k1, k2, v, lam, scale, dk)
        o_ref[rows, :] = (_rms_rows(o, nw_ref[...]) * post).astype(o_ref.dtype)


def _attn_dec(u, row0, cache_k, cache_v, l, rope, lam, norm_w, *, n_seq, L, heads, dk, dv, post):
    past = cache_k.shape[2]
    assert row0 % L == 0
    rb0 = row0 // L
    tq = _tile(L, 256)
    kern = functools.partial(_attn_dec_kernel, dk=dk, dv=dv, past=past, L=L, tq=tq, post=post)
    cos, slo, shi = rope
    tab = pl.BlockSpec((L, dk), lambda b, h: (0, 0))
    return pl.pallas_call(
        kern,
        grid=(n_seq, heads),
        in_specs=[
            pl.BlockSpec(memory_space=pltpu.SMEM),
            pl.BlockSpec((L, 2 * dk), lambda b, h: (rb0 + b, h)),
            pl.BlockSpec((L, 2 * dk), lambda b, h: (rb0 + b, heads + h)),
            pl.BlockSpec((L, dv), lambda b, h: (rb0 + b, 2 * heads + h)),
            pl.BlockSpec((None, None, past, 2 * dk), lambda b, h: (b, l, 0, h)),
            pl.BlockSpec((None, None, past, dv), lambda b, h: (b, l, 0, h)),
            tab, tab, tab,
            pl.BlockSpec((1, dv), lambda b, h: (0, 0)),
        ],
        out_specs=pl.BlockSpec((L, dv), lambda b, h: (b, h)),
        out_shape=jax.ShapeDtypeStruct((n_seq * L, heads * dv), BF16),
        scratch_shapes=[pltpu.VMEM((past + L, 2 * dk), BF16), pltpu.VMEM((past + L, dv), BF16)],
        compiler_params=_cparams(2),
    )(lam, u, u, u, cache_k, cache_v, cos, slo, shi, norm_w)


def _log_sigmoid(x):
    return -(jnp.maximum(-x, 0.0) + jnp.log1p(jnp.exp(-jnp.abs(x))))


def _split3_bf16(x):
    hi = x.astype(BF16)
    r1 = x - hi.astype(F32)
    mid = r1.astype(BF16)
    lo = (r1 - mid.astype(F32)).astype(BF16)
    return hi, mid, lo


def _gla_kernel(*refs, L, K, V, G, has_s0, want_state):
    q_ref, k_ref, v_ref, r_ref, ag_ref, a2_ref, ab_ref, nw_ref = refs[:8]
    refs = refs[8:]
    s0_ref = None
    if has_s0:
        s0_ref, refs = refs[0], refs[1:]
    o_ref, refs = refs[0], refs[1:]
    sfin_ref = None
    if want_state:
        sfin_ref, refs = refs[0], refs[1:]
    st_ref, of_ref, ob_ref = refs

    C = GLA_CHUNK
    N = L // C
    r2 = a2_ref.shape[1]
    row = lax.broadcasted_iota(jnp.int32, (C, C), 0)
    col = lax.broadcasted_iota(jnp.int32, (C, C), 1)
    nt = (((1,), (1,)), ((), ()))
    tn = (((0,), (0,)), ((), ()))
    keeps = (col <= row, col >= row)
    tris = tuple(jnp.where(kp, 1.0, 0.0).astype(BF16) for kp in keeps)

    for d in range(2):
        for g in range(G):
            st_ref[d, g] = s0_ref[d, g].T if has_s0 else jnp.zeros((V, K), F32)

    o_refs = (of_ref, ob_ref)

    def body(i, carry):
        sls, q_in, k_in, k_out, dec = [], [], [], [], []
        for d in range(2):
            n = i if d == 0 else N - 1 - i
            sl = pl.ds(pl.multiple_of(n * C, C), C)
            agc = ag_ref[sl, :][:, :r2].astype(BF16)
            logit = jnp.dot(agc, a2_ref[d].astype(BF16), preferred_element_type=F32) + ab_ref[d]
            la = _log_sigmoid(logit) * (1.0 / GLA_TAU)
            hi, mid, lo = _split3_bf16(la)
            bc = (jnp.dot(tris[d], hi, preferred_element_type=F32) + jnp.dot(tris[d], mid, preferred_element_type=F32)
                  + jnp.dot(tris[d], lo, preferred_element_type=F32))
            blast = bc[C - 1:C, :] if d == 0 else bc[0:1, :]
            kc = k_ref[sl, :]
            sls.append(sl)
            q_in.append((q_ref[sl, :] * (K ** -0.5) * jnp.exp(bc)).astype(BF16))
            k_in.append((kc * jnp.exp(-bc)).astype(BF16))
            k_out.append((kc * jnp.exp(blast - bc)).astype(BF16))
            dec.append(jnp.exp(blast))
        pairs = [(d, g) for d in range(2) for g in range(G)]
        ksl = lambda g: slice(g * K, (g + 1) * K)
        vsl = lambda g: slice(g * V, (g + 1) * V)
        att = {(d, g): lax.dot_general(q_in[d][:, ksl(g)], k_in[d][:, ksl(g)], nt, preferred_element_type=F32)
               for d, g in pairs}
        for d, g in pairs:
            o_refs[d][sls[d], vsl(g)] = lax.dot_general(q_in[d][:, ksl(g)], st_ref[d, g].astype(BF16), nt,
                                                        preferred_element_type=F32)
        for d, g in pairs:
            vc = v_ref[sls[d], vsl(g)].astype(BF16)
            ds_t = lax.dot_general(vc, k_out[d][:, ksl(g)], tn, preferred_element_type=F32)
            st_ref[d, g] = dec[d][:, ksl(g)] * st_ref[d, g] + ds_t
        for d, g in pairs:
            vc = v_ref[sls[d], vsl(g)].astype(BF16)
            a = jnp.where(keeps[d], att[d, g], 0.0).astype(BF16)
            o_refs[d][sls[d], vsl(g)] += jnp.dot(a, vc, preferred_element_type=F32)
        return carry

    lax.fori_loop(0, N, body, 0)
    if want_state:
        for d in range(2):
            for g in range(G):
                sfin_ref[d, g] = st_ref[d, g].T
    for g in range(G):
        vs = slice(g * V, (g + 1) * V)
        og = _rms_rows(of_ref[:, vs] + ob_ref[:, vs], nw_ref[...]) * _silu(r_ref[:, vs])
        o_ref[:, vs] = og.astype(o_ref.dtype)


def _gla(u, ag, row0, a2p, ab, norm_w, s0, *, n_seq, L, heads, K, V, qcol, vcol, rcol, want_state):
    G = _tile(heads, 4)
    gk, gv = G * K, G * V
    assert row0 % L == 0 and qcol % gk == 0 and (heads * K) % gk == 0 and vcol % gv == 0 and rcol % gv == 0
    rb0 = row0 // L
    qb, kb, vb, rb = qcol // gk, (qcol + heads * K) // gk, vcol // gv, rcol // gv
    r2 = a2p.shape[1]
    has_s0 = s0 is not None
    kern = functools.partial(_gla_kernel, L=L, K=K, V=V, G=G, has_s0=has_s0, want_state=want_state)
    in_specs = [
        pl.BlockSpec((L, gk), lambda b, h: (rb0 + b, qb + h)),
        pl.BlockSpec((L, gk), lambda b, h: (rb0 + b, kb + h)),
        pl.BlockSpec((L, gv), lambda b, h: (rb0 + b, vb + h)),
        pl.BlockSpec((L, gv), lambda b, h: (rb0 + b, rb + h)),
        pl.BlockSpec((L, LANE), lambda b, h: (rb0 + b, 0)),
        pl.BlockSpec((2, r2, gk), lambda b, h: (0, 0, h)),
        pl.BlockSpec((2, 1, gk), lambda b, h: (0, 0, h)),
        pl.BlockSpec((1, V), lambda b, h: (0, 0)),
    ]
    args = [u, u, u, u, ag, a2p, ab, norm_w]
    if has_s0:
        in_specs.append(pl.BlockSpec((None, 2, G, K, V), lambda b, h: (b, 0, h, 0, 0)))
        args.append(s0)
    out_specs = [pl.BlockSpec((L, gv), lambda b, h: (b, h))]
    out_shape = [jax.ShapeDtypeStruct((n_seq * L, heads * V), BF16)]
    if want_state:
        out_specs.append(pl.BlockSpec((None, 2, G, K, V), lambda b, h: (b, 0, h, 0, 0)))
        out_shape.append(jax.ShapeDtypeStruct((n_seq, 2, heads, K, V), F32))
    res = pl.pallas_call(
        kern,
        grid=(n_seq, heads // G),
        in_specs=in_specs,
        out_specs=out_specs,
        out_shape=out_shape,
        scratch_shapes=[pltpu.VMEM((2, G, V, K), F32), pltpu.VMEM((L, gv), F32), pltpu.VMEM((L, gv), F32)],
        compiler_params=_cparams(2),
    )(*args)
    return res if want_state else (res[0], None)


def _merge_kernel(odc_ref, odd_ref, ogc_ref, ogd_ref, wd_ref, wg_ref, gd_ref, gg_ref, bd_ref, bg_ref, o_ref,
                  wdb_ref, wgb_ref, *, n_ctx):
    i = pl.program_id(1)

    @pl.when(i == 0)
    def _():
        wdb_ref[...] = wd_ref[...].astype(BF16)
        wgb_ref[...] = wg_ref[...].astype(BF16)

    def run(od_ref, og_ref):
        yd = jnp.dot(od_ref[...], wdb_ref[...], preferred_element_type=F32)
        yg = jnp.dot(og_ref[...], wgb_ref[...], preferred_element_type=F32)
        m = _sigmoid(gd_ref[...] + bd_ref[...]) * yd + _sigmoid(gg_ref[...] + bg_ref[...]) * yg
        o_ref[...] = m.astype(o_ref.dtype)

    @pl.when(i < n_ctx)
    def _():
        run(odc_ref, ogc_ref)

    @pl.when(i >= n_ctx)
    def _():
        run(odd_ref, ogd_ref)


def _merge(od_c, od_d, og_c, og_d, w_da, w_gla, gates, b_gate4, l, *, tm=512, tn=512):
    t_ctx, Kd = od_c.shape
    t_dec = od_d.shape[0]
    Kg = og_c.shape[1]
    D = w_da.shape[2]
    tm = _tile(math.gcd(t_ctx, t_dec), tm)
    tn = _tile(D, tn)
    nj = D // tn
    n_ctx = t_ctx // tm
    T = t_ctx + t_dec
    ctx_rows = lambda j, i: (jnp.minimum(i, n_ctx - 1), 0)
    dec_rows = lambda j, i: (jnp.maximum(i - n_ctx, 0), 0)
    return pl.pallas_call(
        functools.partial(_merge_kernel, n_ctx=n_ctx),
        grid=(nj, T // tm),
        in_specs=[
            pl.BlockSpec((tm, Kd), ctx_rows),
            pl.BlockSpec((tm, Kd), dec_rows),
            pl.BlockSpec((tm, Kg), ctx_rows),
            pl.BlockSpec((tm, Kg), dec_rows),
            pl.BlockSpec((None, Kd, tn), lambda j, i: (l, 0, j)),
            pl.BlockSpec((None, Kg, tn), lambda j, i: (l, 0, j)),
            pl.BlockSpec((tm, tn), lambda j, i: (i, j)),
            pl.BlockSpec((tm, tn), lambda j, i: (i, nj + j)),
            pl.BlockSpec((None, None, 1, tn), lambda j, i: (l, 0, 0, j)),
            pl.BlockSpec((None, None, 1, tn), lambda j, i: (l, 1, 0, j)),
        ],
        out_specs=pl.BlockSpec((tm, tn), lambda j, i: (i, j)),
        out_shape=jax.ShapeDtypeStruct((T, D), BF16),
        scratch_shapes=[pltpu.VMEM((Kd, tn), BF16), pltpu.VMEM((Kg, tn), BF16)],
        compiler_params=_cparams(2),
    )(od_c, od_d, og_c, og_d, w_da, w_gla, gates, gates, b_gate4, b_gate4)


def _layer_norm_rows(z, g, b):
    mu = jnp.mean(z, axis=-1, keepdims=True)
    zc = z - mu
    var = jnp.mean(zc * zc, axis=-1, keepdims=True)
    return zc * lax.rsqrt(var + LN_EPS) * g + b


def _post_mixer_kernel(x_ref, y_ref, g1_ref, sc_ref, sh_ref, lg_ref, lb_ref, wr_ref, br_ref,
                       x1_ref, h2_ref, lo_ref, *, alpha):
    x1 = _layer_norm_rows(alpha * x_ref[...] + g1_ref[...] * y_ref[...], lg_ref[...], lb_ref[...])
    x1_ref[...] = x1
    h2 = x1 * (1.0 + sc_ref[...]) + sh_ref[...]
    h2_ref[...] = h2.reshape(h2_ref.shape)
    lo_ref[...] = jnp.dot(h2, wr_ref[...], preferred_element_type=F32,
                          precision=lax.Precision.HIGHEST) + br_ref[...]


def _post_mixer(x, y, mod5, slab, l, ln_g, ln_b, wr, br, *, alpha):
    T, D = x.shape
    tm = slab.tile(256)
    nr = wr.shape[2]
    row = pl.BlockSpec((tm, D), lambda i: (i, 0))
    vec = pl.BlockSpec((None, 1, D), lambda i: (l, 0, 0))
    return pl.pallas_call(
        functools.partial(_post_mixer_kernel, alpha=alpha),
        grid=(T // tm,),
        in_specs=[row, row, slab.mod_spec(tm, l, 2, D), slab.mod_spec(tm, l, 4, D), slab.mod_spec(tm, l, 3, D),
                  vec, vec,
                  pl.BlockSpec((None, D, nr), lambda i: (l, 0, 0)),
                  pl.BlockSpec((None, 1, nr), lambda i: (l, 0, 0))],
        out_specs=[row, pl.BlockSpec((tm, 1, D), lambda i: (i, 0, 0)), pl.BlockSpec((tm, nr), lambda i: (i, 0))],
        out_shape=[jax.ShapeDtypeStruct((T, D), F32), jax.ShapeDtypeStruct((T, 1, D), F32),
                   jax.ShapeDtypeStruct((T, nr), F32)],
        compiler_params=_cparams(1),
    )(x, y, mod5, mod5, mod5, ln_g, ln_b, wr, br)


def _issue_rows(copy_fn, tm):
    def f(i, c):
        for q in range(ROW_SEMS):
            for cp in copy_fn(i * ROW_SEMS + q, q):
                cp.start(priority=q % 2)
        return c
    lax.fori_loop(0, tm // ROW_SEMS, f, 0, unroll=DMA_UNROLL // ROW_SEMS)


def _wait_rows(src_hbm, dst, sems):
    n = dst.shape[0] // ROW_SEMS
    for q in range(ROW_SEMS):
        pltpu.make_async_copy(src_hbm.at[pl.ds(0, n)], dst.at[pl.ds(0, n)], sems.at[q]).wait()


def _moe_gather_kernel(nused_ref, tok_ref, x_hbm, o_ref, buf_ref, dense_ref, sem, *, tm):
    b = pl.program_id(0)
    nused = nused_ref[0]

    def issue(blk, slot):
        def copy_fn(r, q):
            t = tok_ref[blk * tm + r]
            return (pltpu.make_async_copy(x_hbm.at[pl.ds(t, 1)], buf_ref.at[slot, pl.ds(r, 1)], sem.at[slot, q]),)
        _issue_rows(copy_fn, tm)

    @pl.when(jnp.logical_and(b == 0, nused > 0))
    def _():
        issue(0, 0)

    @pl.when(b + 1 < nused)
    def _():
        issue(b + 1, (b + 1) % 2)

    @pl.when(b < nused)
    def _():
        _wait_rows(x_hbm, buf_ref.at[b % 2], sem.at[b % 2])
        dense_ref[...] = buf_ref[b % 2].reshape(dense_ref.shape)
        o_ref[...] = dense_ref[...].astype(o_ref.dtype)

    @pl.when(b >= nused)
    def _():
        o_ref[...] = jnp.zeros_like(o_ref)


def _moe_gather(h2, nused, buf_tok, *, tm):
    D = h2.shape[2]
    P = buf_tok.shape[0]
    grid_spec = pltpu.PrefetchScalarGridSpec(
        num_scalar_prefetch=2,
        grid=(P // tm,),
        in_specs=[pl.BlockSpec(memory_space=pl.ANY)],
        out_specs=pl.BlockSpec((tm, D), lambda b, nu, tk: (b, 0)),
        scratch_shapes=[pltpu.VMEM((2, tm, 1, D), F32), pltpu.VMEM((tm, D), F32),
                        pltpu.SemaphoreType.DMA((2, ROW_SEMS))],
    )
    return pl.pallas_call(
        functools.partial(_moe_gather_kernel, tm=tm),
        grid_spec=grid_spec,
        out_shape=jax.ShapeDtypeStruct((P, D), BF16),
        compiler_params=_cparams(1),
    )(nused, buf_tok, h2)


def _expert_changed(blk_e_ref, b):
    return jnp.logical_or(b == 0, blk_e_ref[b] != blk_e_ref[jnp.maximum(b - 1, 0)])


def _moe_ffn_kernel(blk_e_ref, nxt_e_ref, nused_ref, xs_ref, w1_hbm, w3_hbm, w2_hbm, o_ref,
                    s1_ref, s3_ref, s2_ref, b1_ref, b3_ref, b2_ref, sem, *, l):
    b = pl.program_id(0)
    used = b < nused_ref[0]
    streams = ((w1_hbm, s1_ref, b1_ref), (w3_hbm, s3_ref, b3_ref), (w2_hbm, s2_ref, b2_ref))

    def fetch(k, e):
        w_hbm, stage_ref, _ = streams[k]
        return pltpu.make_async_copy(w_hbm.at[l, e], stage_ref, sem.at[k])

    @pl.when(jnp.logical_and(used, _expert_changed(blk_e_ref, b)))
    def _():
        e = blk_e_ref[b]
        nxt = nxt_e_ref[b]

        @pl.when(b == 0)
        def _():
            for k in range(3):
                fetch(k, e).start()

        for k in range(3):
            _, stage_ref, cast_ref = streams[k]
            fetch(k, e).wait()
            cast_ref[...] = stage_ref[...].astype(BF16)

            @pl.when(nxt >= 0)
            def _(k=k):
                fetch(k, nxt).start()

    @pl.when(used)
    def _():
        xs = xs_ref[...]
        h1 = jnp.dot(xs, b1_ref[...], preferred_element_type=F32)
        h3 = jnp.dot(xs, b3_ref[...], preferred_element_type=F32)
        hid = (_silu(h1) * h3).astype(BF16)
        o_ref[...] = jnp.dot(hid, b2_ref[...], preferred_element_type=F32)

    @pl.when(jnp.logical_not(used))
    def _():
        o_ref[...] = jnp.zeros_like(o_ref)


def _moe(h2, blk_e, nxt_e, nused, buf_tok, w1, w3, w2, l, *, tm):
    D, FF = w1.shape[2:]
    P = buf_tok.shape[0]
    xs = _moe_gather(h2, nused, buf_tok, tm=tm)
    grid_spec = pltpu.PrefetchScalarGridSpec(
        num_scalar_prefetch=3,
        grid=(P // tm,),
        in_specs=[
            pl.BlockSpec((tm, D), lambda b, be, nx, nu: (b, 0)),
            pl.BlockSpec(memory_space=pl.ANY),
            pl.BlockSpec(memory_space=pl.ANY),
            pl.BlockSpec(memory_space=pl.ANY),
        ],
        out_specs=pl.BlockSpec((tm, D), lambda b, be, nx, nu: (b, 0)),
        scratch_shapes=[pltpu.VMEM((D, FF), F32), pltpu.VMEM((D, FF), F32), pltpu.VMEM((FF, D), F32),
                        pltpu.VMEM((D, FF), BF16), pltpu.VMEM((D, FF), BF16), pltpu.VMEM((FF, D), BF16),
                        pltpu.SemaphoreType.DMA((3,))],
    )
    return pl.pallas_call(
        functools.partial(_moe_ffn_kernel, l=l),
        grid_spec=grid_spec,
        out_shape=jax.ShapeDtypeStruct((P, D), F32),
        compiler_params=_cparams(1),
    )(blk_e, nxt_e, nused, xs, w1, w3, w2)


def _post_moe_kernel(p0_ref, p1_ref, es_hbm, x_ref, gw_ref, g2_ref, lg_ref, lb_ref, *rest, tm, alpha, has_next):
    if has_next:
        sc_ref, sh_ref, x2_ref, h_ref, buf_ref, sem = rest
    else:
        x2_ref, buf_ref, sem = rest
    i = pl.program_id(0)
    n = pl.num_programs(0)
    p_refs = (p0_ref, p1_ref)

    def issue(tile, slot):
        def copy_fn(r, q):
            return tuple(pltpu.make_async_copy(es_hbm.at[pl.ds(p_refs[j][tile * tm + r], 1)],
                                               buf_ref.at[slot, j, pl.ds(r, 1)], sem.at[slot, j, q])
                         for j in range(TOP_K))
        _issue_rows(copy_fn, tm)

    @pl.when(i == 0)
    def _():
        issue(0, 0)

    @pl.when(i + 1 < n)
    def _():
        issue(i + 1, (i + 1) % 2)

    slot = i % 2
    for j in range(TOP_K):
        _wait_rows(es_hbm, buf_ref.at[slot, j], sem.at[slot, j])
    y = gw_ref[:, 0:1] * buf_ref[slot, 0] + gw_ref[:, 1:2] * buf_ref[slot, 1]
    x2 =_layer_norm_rows(alpha * x_ref[...] + g2_ref[...] * y, lg_ref[...], lb_ref[...])
    x2_ref[...] = x2
    if has_next:
        h_ref[...] = (x2 * (1.0 + sc_ref[...]) + sh_ref[...]).astype(h_ref.dtype)


def _post_moe(p0, p1, es, x1, gate, mod5, slab, l, ln_g, ln_b, *, alpha, has_next):
    T, D = x1.shape
    tm = slab.tile(256)
    row = pl.BlockSpec((tm, D), lambda i, *_: (i, 0))
    vec = pl.BlockSpec((None, 1, D), lambda i, *_: (l, 0, 0))
    in_specs = [pl.BlockSpec(memory_space=pl.ANY), row, pl.BlockSpec((tm, TOP_K), lambda i, *_: (i, 0)),
                slab.mod_spec(tm, l, 5, D), vec, vec]
    args = [es, x1, gate, mod5, ln_g, ln_b]
    out_specs = [row]
    out_shape = [jax.ShapeDtypeStruct((T, D), F32)]
    if has_next:
        in_specs += [slab.mod_spec(tm, l + 1, 1, D), slab.mod_spec(tm, l + 1, 0, D)]
        args += [mod5, mod5]
        out_specs.append(row)
        out_shape.append(jax.ShapeDtypeStruct((T, D), BF16))
    grid_spec = pltpu.PrefetchScalarGridSpec(
        num_scalar_prefetch=2,
        grid=(T // tm,),
        in_specs=in_specs,
        out_specs=out_specs,
        scratch_shapes=[pltpu.VMEM((2, TOP_K, tm, D), F32), pltpu.SemaphoreType.DMA((2, TOP_K, ROW_SEMS))],
    )
    res = pl.pallas_call(
        functools.partial(_post_moe_kernel, tm=tm, alpha=alpha, has_next=has_next),
        grid_spec=grid_spec,
        out_shape=out_shape,
        compiler_params=_cparams(1),
    )(p0, p1, *args)
    return (res[0], res[1]) if has_next else (res[0], None)


def _route(logits, n_groups, epg, tm):
    T = logits.shape[0]
    n_exp = n_groups * epg
    pg = jax.nn.softmax(logits[:, :n_groups], axis=-1)
    g_idx = jnp.argmax(pg, axis=-1)
    p_sel = jnp.take_along_axis(pg, g_idx[:, None], axis=-1)
    le = logits[:, n_groups:n_groups + n_exp].reshape(T, n_groups, epg)
    le = jnp.take_along_axis(le, g_idx[:, None, None], axis=1)[:, 0]
    pe = jax.nn.softmax(le, axis=-1)
    top_p, top_i = lax.top_k(pe, TOP_K)
    gate = p_sel * top_p / jnp.sum(top_p, axis=-1, keepdims=True)
    expert = (g_idx[:, None] * epg + top_i).astype(jnp.int32)

    A = T * TOP_K
    e_flat = expert.reshape(A)
    tok_flat = jnp.repeat(jnp.arange(T, dtype=jnp.int32), TOP_K)
    onehot = (e_flat[:, None] == jnp.arange(n_exp, dtype=jnp.int32)[None, :]).astype(jnp.int32)
    csum = jnp.cumsum(onehot, axis=0)
    counts = csum[-1]
    rank = jnp.take_along_axis(csum, e_flat[:, None], axis=1)[:, 0] - 1
    padded = (counts + tm - 1) // tm * tm
    pends = jnp.cumsum(padded)
    pstarts = pends - padded
    pos = (pstarts[e_flat] + rank).astype(jnp.int32)
    n_blocks = -(-A // tm) + n_exp
    P = n_blocks * tm
    buf_tok = jnp.zeros((P,), jnp.int32).at[pos].set(tok_flat)
    blk_e = jnp.clip(jnp.searchsorted(pends, jnp.arange(n_blocks, dtype=pends.dtype) * tm, side='right'),
                     0, n_exp - 1).astype(jnp.int32)
    nused = (pends[-1] // tm).astype(jnp.int32).reshape(1)
    ids = jnp.arange(n_exp, dtype=jnp.int32)
    own = jnp.where(counts > 0, ids, n_exp)
    later = lax.cummin(jnp.concatenate([own[1:], jnp.full((1,), n_exp, jnp.int32)]), reverse=True)
    nxt_e = jnp.where(later < n_exp, later, -1)[blk_e].astype(jnp.int32)
    pos2 = pos.reshape(T, TOP_K)
    return blk_e, nxt_e, nused, buf_tok, gate.astype(F32), pos2[:, 0], pos2[:, 1]


def _rope_tables(n_tokens, dk):
    freqs = dk // 4
    rows = n_tokens // GRID_W
    row = jnp.repeat(jnp.arange(rows, dtype=F32), GRID_W)
    col = jnp.tile(jnp.arange(GRID_W, dtype=F32), rows)
    inv = 1.0 / (ROPE_THETA ** (jnp.arange(freqs, dtype=F32) / freqs))
    ar, ac = row[:, None] * inv, col[:, None] * inv
    z = jnp.zeros_like(ar)
    cos = jnp.concatenate([jnp.cos(ar), jnp.cos(ar), jnp.cos(ac), jnp.cos(ac)], axis=1)
    s_lo = jnp.concatenate([-jnp.sin(ar), z, -jnp.sin(ac), z], axis=1)
    s_hi = jnp.concatenate([z, jnp.sin(ar), z, jnp.sin(ac)], axis=1)
    return cos, s_lo, s_hi


def kernel(x_prompt, x_sample, cache_k, cache_v, state_gla, c, c_ctx, w_mod, b_mod, w_in, b_gate, da_lambda,
           da_norm_w, gla_a2, gla_a_bias, gla_norm_w, w_da_proj, w_gla_proj, w_out, ln1_g, ln1_b, ln2_g, ln2_b,
           router_g_w, router_g_b, router_e_w, router_e_b, exp_w1, exp_w3, exp_w2):
    B, S, D = x_prompt.shape
    n_dec, dec_seq, _ = x_sample.shape
    depth = w_mod.shape[0]
    past, da_heads, _, dk = cache_k.shape[2:]
    dv = cache_v.shape[-1]
    gla_heads, gk, gv = state_gla.shape[3:]
    lowrank = gla_a2.shape[2]
    n_groups, epg = router_e_w.shape[2:]
    n_exp = n_groups * epg
    da_qw, da_vw = da_heads * 2 * dk, da_heads * dv
    gla_qw, gla_vw = gla_heads * gk, gla_heads * gv
    n_main = 2 * da_qw + da_vw + 2 * gla_qw + 2 * gla_vw
    assert n_main % LANE == 0 and w_in.shape[2] == n_main + 2 * lowrank + 2 * D
    alpha = (2.0 * depth) ** 0.25

    slab = _Slab(B * S, dec_seq, n_dec)
    t_ctx = slab.t_ctx
    moe_tm = 256

    x = jnp.concatenate([x_prompt.reshape(t_ctx, D), x_sample.reshape(n_dec * dec_seq, D)], axis=0)
    n_cond = 8
    cond = jnp.zeros((n_cond, D), F32).at[0].set(c_ctx).at[1:1 + n_dec].set(c)
    w_in_t = jnp.swapaxes(w_in, 1, 2)
    b_gate4 = b_gate.reshape(depth, 2, 1, D)
    nr =-(-(n_groups + n_exp) // LANE) * LANE
    wr = jnp.concatenate([router_g_w, router_e_w.reshape(depth, D, n_exp),
                          jnp.zeros((depth, D, nr - n_groups - n_exp), F32)], axis=2)
    br = jnp.concatenate([router_g_b, router_e_b.reshape(depth, n_exp),
                          jnp.zeros((depth, nr - n_groups - n_exp), F32)], axis=1).reshape(depth, 1, nr)
    a2p = jnp.zeros((depth, 2, 2 * lowrank, gla_qw), F32)
    a2p = a2p.at[:, 0, :lowrank].set(gla_a2[:, 0]).at[:, 1, lowrank:].set(gla_a2[:, 1])
    ab = gla_a_bias.reshape(depth, 2, 1, gla_qw)
    lv = da_lambda.astype(F32)
    lam_init = [0.8 - 0.6 * math.exp(-0.3 * l) for l in range(depth)]
    lam = [(jnp.exp(jnp.sum(lv[l, 0] * lv[l, 1])) - jnp.exp(jnp.sum(lv[l, 2] * lv[l, 3])) + lam_init[l]).reshape(1, 1)
           for l in range(depth)]
    rope = _rope_tables(dec_seq, dk)
    cache_k2 = cache_k.reshape(n_dec, depth, past, da_qw)
    cache_v2 = cache_v.reshape(n_dec, depth, past, da_vw)
    ln1_g3, ln1_b3 = ln1_g.reshape(depth, 1, D), ln1_b.reshape(depth, 1, D)
    ln2_g3, ln2_b3 = ln2_g.reshape(depth, 1, D), ln2_b.reshape(depth, 1, D)

    mod = _modulation(cond, w_mod, b_mod)
    mod5 = mod.reshape(depth, n_cond, 6, 1, D)

    h = _premod(x, mod5, slab, 0)
    ks, vs, sts = [], [], []
    for l in range(depth):
        u = _matmul_t(h, w_in_t, l, 0, n_main)
        ag = _matmul_t(h, w_in_t, l, n_main, LANE, tn=LANE)
        gates = _matmul_t(h, w_in_t, l, n_main + 2 * lowrank, 2 * D)
        ks.append(u[:t_ctx, da_qw:2 * da_qw].reshape(B, S, da_qw))
        vs.append(u[:t_ctx, 2 * da_qw:2 * da_qw + da_vw].reshape(B, S, da_vw))

        post = 1.0 - lam_init[l]
        nw_d = da_norm_w[l].reshape(1, dv)
        od_c = _attn_ctx(u, lam[l], nw_d, n_seq=B, L=S, heads=da_heads, dk=dk, dv=dv, post=post)
        od_d = _attn_dec(u, t_ctx, cache_k2, cache_v2, l, rope, lam[l], nw_d,
                         n_seq=n_dec, L=dec_seq, heads=da_heads, dk=dk, dv=dv, post=post)

        qcol = 2 * da_qw + da_vw
        vcol = qcol + 2 * gla_qw
        rcol = vcol + gla_vw
        nw_g = gla_norm_w[l].reshape(1, gv)
        gla_kw = dict(heads=gla_heads, K=gk, V=gv, qcol=qcol, vcol=vcol, rcol=rcol)
        og_c, st = _gla(u, ag, 0, a2p[l], ab[l], nw_g, None, n_seq=B, L=S, want_state=True, **gla_kw)
        og_d, _ = _gla(u, ag, t_ctx, a2p[l], ab[l], nw_g, state_gla[:, l], n_seq=n_dec, L=dec_seq,
                       want_state=False, **gla_kw)
        sts.append(st)

        m = _merge(od_c, od_d, og_c, og_d, w_da_proj, w_gla_proj, gates, b_gate4, l)
        y = _matmul(m, w_out, l, 0, D)
        x1, h2, logits = _post_mixer(x, y, mod5, slab, l, ln1_g3, ln1_b3, wr, br, alpha=alpha)

        blk_e, nxt_e, nused, buf_tok, gate, p0, p1 = _route(logits, n_groups, epg, moe_tm)
        es = _moe(h2, blk_e, nxt_e, nused, buf_tok, exp_w1, exp_w3, exp_w2, l, tm=moe_tm)
        x, h = _post_moe(p0, p1, es, x1, gate, mod5, slab, l, ln2_g3, ln2_b3, alpha=alpha, has_next=l + 1 < depth)

    y_prompt = x[:t_ctx].reshape(B, S, D)
    y_sample = x[t_ctx:].reshape(n_dec, dec_seq, D)
    new_k = jnp.stack(ks, axis=1).reshape(B, depth, S, da_heads, 2, dk)
    new_v = jnp.stack(vs, axis=1).reshape(B, depth, S, da_heads, dv)
    return (y_prompt, y_sample, new_k, new_v, jnp.stack(sts, axis=1))
```

```python
import functools
import math

import jax
import jax.numpy as jnp
from jax import lax
from jax.experimental import pallas as pl
from jax.experimental.pallas import tpu as pltpu

F32 = jnp.float32
BF16 = jnp.bfloat16

GRID_W = 64
ROPE_THETA = 10000.0
GLA_TAU = 16.0
GLA_CHUNK = 64
TOP_K = 2
LN_EPS = 1e-5
LANE = 128
VMEM_LIMIT = 56 * 1024 * 1024
DMA_UNROLL = 8
ROW_SEMS = 4


def _cparams(n_axes):
    return pltpu.CompilerParams(dimension_semantics=("arbitrary",) * n_axes, vmem_limit_bytes=VMEM_LIMIT)


def _tile(n, pref):
    t = min(n, pref)
    while n % t:
        t //= 2
    return t


def _silu(x):
    return x * (1.0 / (1.0 + jnp.exp(-x)))


def _sigmoid(x):
    return 1.0 / (1.0 + jnp.exp(-x))


def _mm_kernel(x_ref, w_ref, o_ref, wb_ref):
    @pl.when(pl.program_id(1) == 0)
    def _():
        wb_ref[...] = w_ref[...].astype(BF16)

    o_ref[...] = jnp.dot(x_ref[...], wb_ref[...], preferred_element_type=F32).astype(o_ref.dtype)


def _matmul(x, w, l, col0, ncols, *, tm=1024, tn=512, out_dtype=F32):
    M, K = x.shape
    tm = _tile(M, tm)
    tn = _tile(ncols, tn)
    assert col0 % tn == 0
    cb0 = col0 // tn
    return pl.pallas_call(
        _mm_kernel,
        grid=(ncols // tn, M // tm),
        in_specs=[
            pl.BlockSpec((tm, K), lambda j, i: (i, 0)),
            pl.BlockSpec((None, K, tn), lambda j, i: (l, 0, cb0 + j)),
        ],
        out_specs=pl.BlockSpec((tm, tn), lambda j, i: (i, j)),
        out_shape=jax.ShapeDtypeStruct((M, ncols), out_dtype),
        scratch_shapes=[pltpu.VMEM((K, tn), BF16)],
        compiler_params=_cparams(2),
    )(x, w)


_NT = (((1,), (1,)), ((), ()))


def _mm_t_kernel(x_ref, w_ref, o_ref, wb_ref):
    @pl.when(pl.program_id(1) == 0)
    def _():
        wb_ref[...] = w_ref[...].astype(BF16)

    o_ref[...] = lax.dot_general(x_ref[...], wb_ref[...], _NT, preferred_element_type=F32).astype(o_ref.dtype)


def _mm_t_shift_kernel(x_ref, wa_ref, wt_ref, o_ref, wb_ref, *, shift):
    @pl.when(pl.program_id(1) == 0)
    def _():
        tn = wb_ref.shape[0]
        wb_ref[0:tn - shift, :] = wa_ref[shift:, :].astype(BF16)
        wb_ref[tn - shift:, :] = wt_ref[:shift, :].astype(BF16)

    o_ref[...] = lax.dot_general(x_ref[...], wb_ref[...], _NT, preferred_element_type=F32).astype(o_ref.dtype)


def _matmul_t(x, wt, l, row0, nrows, *, tm=1024, tn=512, out_dtype=F32):
    M, K = x.shape
    tm = _tile(M, tm)
    tn = _tile(nrows, tn)
    base = row0 // LANE * LANE
    shift = row0 - base
    assert base % tn == 0 and tn % LANE == 0 and shift % 32 == 0
    rb0 = base // tn
    in_specs = [
        pl.BlockSpec((tm, K), lambda j, i: (i, 0)),
        pl.BlockSpec((None, tn, K), lambda j, i: (l, rb0 + j, 0)),
    ]
    args = [x, wt]
    kern = _mm_t_kernel
    if shift:
        lpt = tn // LANE
        in_specs.append(pl.BlockSpec((None, LANE, K), lambda j, i: (l, (rb0 + j + 1) * lpt, 0)))
        args.append(wt)
        kern = functools.partial(_mm_t_shift_kernel, shift=shift)
    return pl.pallas_call(
        kern,
        grid=(nrows // tn, M // tm),
        in_specs=in_specs,
        out_specs=pl.BlockSpec((tm, tn), lambda j, i: (i, j)),
        out_shape=jax.ShapeDtypeStruct((M, nrows), out_dtype),
        scratch_shapes=[pltpu.VMEM((tn, K), BF16)],
        compiler_params=_cparams(2),
    )(*args)


def _mod_kernel(c_ref, w_ref, b_ref, o_ref):
    a = _silu(c_ref[...]).astype(BF16)
    o_ref[...] = jnp.dot(a, w_ref[...].astype(BF16), preferred_element_type=F32) + b_ref[...]


def _modulation(cond, w_mod, b_mod, *, tn=512):
    depth, D, N = w_mod.shape
    R = cond.shape[0]
    tn = _tile(N, tn)
    return pl.pallas_call(
        _mod_kernel,
        grid=(depth, N // tn),
        in_specs=[
            pl.BlockSpec((R, D), lambda l, j: (0, 0)),
            pl.BlockSpec((None, D, tn), lambda l, j: (l, 0, j)),
            pl.BlockSpec((None, 1, tn), lambda l, j: (l, 0, j)),
        ],
        out_specs=pl.BlockSpec((None, R, tn), lambda l, j: (l, 0, j)),
        out_shape=jax.ShapeDtypeStruct((depth, R, N), F32),
        compiler_params=_cparams(2),
    )(cond, w_mod, b_mod.reshape(depth, 1, N))


class _Slab:
    def __init__(self, t_ctx, dec_seq, n_dec):
        self.t_ctx, self.dec_seq, self.n_dec = t_ctx, dec_seq, n_dec
        self.T = t_ctx + dec_seq * n_dec

    def tile(self, pref):
        t = min(pref, self.t_ctx, self.dec_seq)
        while self.t_ctx % t or self.dec_seq % t:
            t //= 2
        return t

    def mod_spec(self, tm, l, which, D):
        n_ctx = self.t_ctx // tm
        per = self.dec_seq // tm

        def imap(i, *_):
            return (l, jnp.where(i < n_ctx, 0, (i - n_ctx) // per + 1), which, 0, 0)

        return pl.BlockSpec((None, None, None, 1, D), imap)


def _premod_kernel(x_ref, sc_ref, sh_ref, h_ref):
    h_ref[...] = (x_ref[...] * (1.0 + sc_ref[...]) + sh_ref[...]).astype(h_ref.dtype)


def _premod(x, mod5, slab, l):
    T, D = x.shape
    tm = slab.tile(256)
    return pl.pallas_call(
        _premod_kernel,
        grid=(T // tm,),
        in_specs=[pl.BlockSpec((tm, D), lambda i: (i, 0)), slab.mod_spec(tm, l, 1, D), slab.mod_spec(tm, l, 0, D)],
        out_specs=pl.BlockSpec((tm, D), lambda i: (i, 0)),
        out_shape=jax.ShapeDtypeStruct((T, D), BF16),
        compiler_params=_cparams(1),
    )(x, mod5, mod5)


def _softmax_rows(s):
    m = jnp.max(s, axis=-1, keepdims=True)
    e = jnp.exp(s - m)
    return e / jnp.sum(e, axis=-1, keepdims=True)


def _diff_attn_rows(q, k1, k2, v, lam, scale, dk):
    nt = (((1,), (1,)), ((), ()))
    s1 = lax.dot_general(q[:, :dk].astype(BF16), k1, nt, preferred_element_type=F32) * scale
    s2 = lax.dot_general(q[:, dk:].astype(BF16), k2, nt, preferred_element_type=F32) * scale
    a = _softmax_rows(s1) - lam * _softmax_rows(s2)
    return jnp.dot(a.astype(BF16), v, preferred_element_type=F32)


def _rms_rows(o, w):
    return o * lax.rsqrt(jnp.mean(o * o, axis=-1, keepdims=True) + LN_EPS) * w


def _attn_ctx_kernel(lam_ref, q_ref, k_ref, v_ref, nw_ref, o_ref, *, heads, dk, dv, post):
    lam = lam_ref[0, 0]
    scale = dk ** -0.5
    for h in range(heads):
        q = q_ref[:, h * 2 * dk:(h + 1) * 2 * dk]
        k = k_ref[:, h * 2 * dk:(h + 1) * 2 * dk].astype(BF16)
        v = v_ref[:, h * dv:(h + 1) * dv].astype(BF16)
        o = _diff_attn_rows(q, k[:, :dk], k[:, dk:], v, lam, scale, dk)
        o_ref[:, h * dv:(h + 1) * dv] = (_rms_rows(o, nw_ref[...]) * post).astype(o_ref.dtype)


def _attn_ctx(u, lam, norm_w, *, n_seq, L, heads, dk, dv, post):
    qw = heads * 2 * dk
    vw = heads * dv
    assert qw == vw
    kern = functools.partial(_attn_ctx_kernel, heads=heads, dk=dk, dv=dv, post=post)
    return pl.pallas_call(
        kern,
        grid=(n_seq,),
        in_specs=[
            pl.BlockSpec(memory_space=pltpu.SMEM),
            pl.BlockSpec((L, qw), lambda b: (b, 0)),
            pl.BlockSpec((L, qw), lambda b: (b, 1)),
            pl.BlockSpec((L, vw), lambda b: (b, 2)),
            pl.BlockSpec((1, dv), lambda b: (0, 0)),
        ],
        out_specs=pl.BlockSpec((L, vw), lambda b: (b, 0)),
        out_shape=jax.ShapeDtypeStruct((n_seq * L, vw), BF16),
        compiler_params=_cparams(1),
    )(lam, u, u, u, norm_w)


def _rope_rows(x, cos, s_lo, s_hi, dk):
    q4 = dk // 4
    return x * cos + pltpu.roll(x, dk - q4, 1) * s_lo + pltpu.roll(x, q4, 1) * s_hi


def _attn_dec_kernel(lam_ref, q_ref, k_ref, v_ref, ck_ref, cv_ref, cos_ref, slo_ref, shi_ref, nw_ref, o_ref,
                     kb_ref, vb_ref, *, dk, dv, past, L, tq, post):
    lam = lam_ref[0, 0]
    scale = dk ** -0.5
    cos, slo, shi = cos_ref[...], slo_ref[...], shi_ref[...]
    kb_ref[0:past, :] = ck_ref[...].astype(BF16)
    vb_ref[0:past, :] = cv_ref[...].astype(BF16)
    for c in range(2):
        kb_ref[past:past + L, c * dk:(c + 1) * dk] = _rope_rows(
            k_ref[:, c * dk:(c + 1) * dk], cos, slo, shi, dk).astype(BF16)
    vb_ref[past:past + L, :] = v_ref[...].astype(BF16)
    k1 = kb_ref[:, :dk]
    k2 = kb_ref[:, dk:]
    v = vb_ref[...]
    for i in range(L // tq):
        rows = slice(i * tq, (i + 1) * tq)
        q = jnp.concatenate(
            [_rope_rows(q_ref[rows, c * dk:(c + 1) * dk], cos[rows], slo[rows], shi[rows], dk) for c in range(2)],
            axis=1)
        o = _diff_attn_rows(q, k1, k2, v, lam, scale, dk)
        o_ref[rows, :] = (_rms_rows(o, nw_ref[...]) * post).astype(o_ref.dtype)


def _attn_dec(u, row0, cache_k, cache_v, l, rope, lam, norm_w, *, n_seq, L, heads, dk, dv, post):
    past = cache_k.shape[2]
    assert row0 % L == 0
    rb0 = row0 // L
    tq = _tile(L, 256)
    kern = functools.partial(_attn_dec_kernel, dk=dk, dv=dv, past=past, L=L, tq=tq, post=post)
    cos, slo, shi = rope
    tab = pl.BlockSpec((L, dk), lambda b, h: (0, 0))
    return pl.pallas_call(
        kern,
        grid=(n_seq, heads),
        in_specs=[
            pl.BlockSpec(memory_space=pltpu.SMEM),
            pl.BlockSpec((L, 2 * dk), lambda b, h: (rb0 + b, h)),
            pl.BlockSpec((L, 2 * dk), lambda b, h: (rb0 + b, heads + h)),
            pl.BlockSpec((L, dv), lambda b, h: (rb0 + b, 2 * heads + h)),
            pl.BlockSpec((None, None, past, 2 * dk), lambda b, h: (b, l, 0, h)),
            pl.BlockSpec((None, None, past, dv), lambda b, h: (b, l, 0, h)),
            tab, tab, tab,
            pl.BlockSpec((1, dv), lambda b, h: (0, 0)),
        ],
        out_specs=pl.BlockSpec((L, dv), lambda b, h: (b, h)),
        out_shape=jax.ShapeDtypeStruct((n_seq * L, heads * dv), BF16),
        scratch_shapes=[pltpu.VMEM((past + L, 2 * dk), BF16), pltpu.VMEM((past + L, dv), BF16)],
        compiler_params=_cparams(2),
    )(lam, u, u, u, cache_k, cache_v, cos, slo, shi, norm_w)


def _log_sigmoid(x):
    return -(jnp.maximum(-x, 0.0) + jnp.log1p(jnp.exp(-jnp.abs(x))))


def _split3_bf16(x):
    hi = x.astype(BF16)
    r1 = x - hi.astype(F32)
    mid = r1.astype(BF16)
    lo = (r1 - mid.astype(F32)).astype(BF16)
    return hi, mid, lo


def _gla_kernel(*refs, L, K, V, G, has_s0, want_state):
    q_ref, k_ref, v_ref, r_ref, ag_ref, a2_ref, ab_ref, nw_ref = refs[:8]
    refs = refs[8:]
    s0_ref = None
    if has_s0:
        s0_ref, refs = refs[0], refs[1:]
    o_ref, refs = refs[0], refs[1:]
    sfin_ref = None
    if want_state:
        sfin_ref, refs = refs[0], refs[1:]
    st_ref, of_ref, ob_ref = refs

    C = GLA_CHUNK
    N = L // C
    r2 = a2_ref.shape[1]
    row = lax.broadcasted_iota(jnp.int32, (C, C), 0)
    col = lax.broadcasted_iota(jnp.int32, (C, C), 1)
    nt = (((1,), (1,)), ((), ()))
    tn = (((0,), (0,)), ((), ()))
    keeps = (col <= row, col >= row)
    tris = tuple(jnp.where(kp, 1.0, 0.0).astype(BF16) for kp in keeps)

    for d in range(2):
        for g in range(G):
            st_ref[d, g] = s0_ref[d, g].T if has_s0 else jnp.zeros((V, K), F32)

    o_refs = (of_ref, ob_ref)

    def body(i, carry):
        sls, q_in, k_in, k_out, dec = [], [], [], [], []
        for d in range(2):
            n = i if d == 0 else N - 1 - i
            sl = pl.ds(pl.multiple_of(n * C, C), C)
            agc = ag_ref[sl, :][:, :r2].astype(BF16)
            logit = jnp.dot(agc, a2_ref[d].astype(BF16), preferred_element_type=F32) + ab_ref[d]
            la = _log_sigmoid(logit) * (1.0 / GLA_TAU)
            hi, mid, lo = _split3_bf16(la)
            bc = (jnp.dot(tris[d], hi, preferred_element_type=F32) + jnp.dot(tris[d], mid, preferred_element_type=F32)
                  + jnp.dot(tris[d], lo, preferred_element_type=F32))
            blast = bc[C - 1:C, :] if d == 0 else bc[0:1, :]
            kc = k_ref[sl, :]
            sls.append(sl)
            q_in.append((q_ref[sl, :] * (K ** -0.5) * jnp.exp(bc)).astype(BF16))
            k_in.append((kc * jnp.exp(-bc)).astype(BF16))
            k_out.append((kc * jnp.exp(blast - bc)).astype(BF16))
            dec.append(jnp.exp(blast))
        pairs = [(d, g) for d in range(2) for g in range(G)]
        ksl = lambda g: slice(g * K, (g + 1) * K)
        vsl = lambda g: slice(g * V, (g + 1) * V)
        att = {(d, g): lax.dot_general(q_in[d][:, ksl(g)], k_in[d][:, ksl(g)], nt, preferred_element_type=F32)
               for d, g in pairs}
        for d, g in pairs:
            o_refs[d][sls[d], vsl(g)] = lax.dot_general(q_in[d][:, ksl(g)], st_ref[d, g].astype(BF16), nt,
                                                        preferred_element_type=F32)
        for d, g in pairs:
            vc = v_ref[sls[d], vsl(g)].astype(BF16)
            ds_t = lax.dot_general(vc, k_out[d][:, ksl(g)], tn, preferred_element_type=F32)
            st_ref[d, g] = dec[d][:, ksl(g)] * st_ref[d, g] + ds_t
        for d, g in pairs:
            vc = v_ref[sls[d], vsl(g)].astype(BF16)
            a = jnp.where(keeps[d], att[d, g], 0.0).astype(BF16)
            o_refs[d][sls[d], vsl(g)] += jnp.dot(a, vc, preferred_element_type=F32)
        return carry

    lax.fori_loop(0, N, body, 0)
    if want_state:
        for d in range(2):
            for g in range(G):
                sfin_ref[d, g] = st_ref[d, g].T
    for g in range(G):
        vs = slice(g * V, (g + 1) * V)
        og = _rms_rows(of_ref[:, vs] + ob_ref[:, vs], nw_ref[...]) * _silu(r_ref[:, vs])
        o_ref[:, vs] = og.astype(o_ref.dtype)


def _gla(u, ag, row0, a2p, ab, norm_w, s0, *, n_seq, L, heads, K, V, qcol, vcol, rcol, want_state):
    G = _tile(heads, 4)
    gk, gv = G * K, G * V
    assert row0 % L == 0 and qcol % gk == 0 and (heads * K) % gk == 0 and vcol % gv == 0 and rcol % gv == 0
    rb0 = row0 // L
    qb, kb, vb, rb = qcol // gk, (qcol + heads * K) // gk, vcol // gv, rcol // gv
    r2 = a2p.shape[1]
    has_s0 = s0 is not None
    kern = functools.partial(_gla_kernel, L=L, K=K, V=V, G=G, has_s0=has_s0, want_state=want_state)
    in_specs = [
        pl.BlockSpec((L, gk), lambda b, h: (rb0 + b, qb + h)),
        pl.BlockSpec((L, gk), lambda b, h: (rb0 + b, kb + h)),
        pl.BlockSpec((L, gv), lambda b, h: (rb0 + b, vb + h)),
        pl.BlockSpec((L, gv), lambda b, h: (rb0 + b, rb + h)),
        pl.BlockSpec((L, LANE), lambda b, h: (rb0 + b, 0)),
        pl.BlockSpec((2, r2, gk), lambda b, h: (0, 0, h)),
        pl.BlockSpec((2, 1, gk), lambda b, h: (0, 0, h)),
        pl.BlockSpec((1, V), lambda b, h: (0, 0)),
    ]
    args = [u, u, u, u, ag, a2p, ab, norm_w]
    if has_s0:
        in_specs.append(pl.BlockSpec((None, 2, G, K, V), lambda b, h: (b, 0, h, 0, 0)))
        args.append(s0)
    out_specs = [pl.BlockSpec((L, gv), lambda b, h: (b, h))]
    out_shape = [jax.ShapeDtypeStruct((n_seq * L, heads * V), BF16)]
    if want_state:
        out_specs.append(pl.BlockSpec((None, 2, G, K, V), lambda b, h: (b, 0, h, 0, 0)))
        out_shape.append(jax.ShapeDtypeStruct((n_seq, 2, heads, K, V), F32))
    res = pl.pallas_call(
        kern,
        grid=(n_seq, heads // G),
        in_specs=in_specs,
        out_specs=out_specs,
        out_shape=out_shape,
        scratch_shapes=[pltpu.VMEM((2, G, V, K), F32), pltpu.VMEM((L, gv), F32), pltpu.VMEM((L, gv), F32)],
        compiler_params=_cparams(2),
    )(*args)
    return res if want_state else (res[0], None)


def _merge_kernel(odc_ref, odd_ref, ogc_ref, ogd_ref, wd_ref, wg_ref, gd_ref, gg_ref, bd_ref, bg_ref, o_ref,
                  wdb_ref, wgb_ref, *, n_ctx):
    i = pl.program_id(1)

    @pl.when(i == 0)
    def _():
        wdb_ref[...] = wd_ref[...].astype(BF16)
        wgb_ref[...] = wg_ref[...].astype(BF16)

    def run(od_ref, og_ref):
        yd = jnp.dot(od_ref[...], wdb_ref[...], preferred_element_type=F32)
        yg = jnp.dot(og_ref[...], wgb_ref[...], preferred_element_type=F32)
        m = _sigmoid(gd_ref[...] + bd_ref[...]) * yd + _sigmoid(gg_ref[...] + bg_ref[...]) * yg
        o_ref[...] = m.astype(o_ref.dtype)

    @pl.when(i < n_ctx)
    def _():
        run(odc_ref, ogc_ref)

    @pl.when(i >= n_ctx)
    def _():
        run(odd_ref, ogd_ref)


def _merge(od_c, od_d, og_c, og_d, w_da, w_gla, gates, b_gate4, l, *, tm=512, tn=512):
    t_ctx, Kd = od_c.shape
    t_dec = od_d.shape[0]
    Kg = og_c.shape[1]
    D = w_da.shape[2]
    tm = _tile(math.gcd(t_ctx, t_dec), tm)
    tn = _tile(D, tn)
    nj = D // tn
    n_ctx = t_ctx // tm
    T = t_ctx + t_dec
    ctx_rows = lambda j, i: (jnp.minimum(i, n_ctx - 1), 0)
    dec_rows = lambda j, i: (jnp.maximum(i - n_ctx, 0), 0)
    return pl.pallas_call(
        functools.partial(_merge_kernel, n_ctx=n_ctx),
        grid=(nj, T // tm),
        in_specs=[
            pl.BlockSpec((tm, Kd), ctx_rows),
            pl.BlockSpec((tm, Kd), dec_rows),
            pl.BlockSpec((tm, Kg), ctx_rows),
            pl.BlockSpec((tm, Kg), dec_rows),
            pl.BlockSpec((None, Kd, tn), lambda j, i: (l, 0, j)),
            pl.BlockSpec((None, Kg, tn), lambda j, i: (l, 0, j)),
            pl.BlockSpec((tm, tn), lambda j, i: (i, j)),
            pl.BlockSpec((tm, tn), lambda j, i: (i, nj + j)),
            pl.BlockSpec((None, None, 1, tn), lambda j, i: (l, 0, 0, j)),
            pl.BlockSpec((None, None, 1, tn), lambda j, i: (l, 1, 0, j)),
        ],
        out_specs=pl.BlockSpec((tm, tn), lambda j, i: (i, j)),
        out_shape=jax.ShapeDtypeStruct((T, D), BF16),
        scratch_shapes=[pltpu.VMEM((Kd, tn), BF16), pltpu.VMEM((Kg, tn), BF16)],
        compiler_params=_cparams(2),
    )(od_c, od_d, og_c, og_d, w_da, w_gla, gates, gates, b_gate4, b_gate4)


def _layer_norm_rows(z, g, b):
    mu = jnp.mean(z, axis=-1, keepdims=True)
    zc = z - mu
    var = jnp.mean(zc * zc, axis=-1, keepdims=True)
    return zc * lax.rsqrt(var + LN_EPS) * g + b


def _post_mixer_kernel(x_ref, y_ref, g1_ref, sc_ref, sh_ref, lg_ref, lb_ref, wr_ref, br_ref,
                       x1_ref, h2_ref, lo_ref, *, alpha):
    x1 = _layer_norm_rows(alpha * x_ref[...] + g1_ref[...] * y_ref[...], lg_ref[...], lb_ref[...])
    x1_ref[...] = x1
    h2 = x1 * (1.0 + sc_ref[...]) + sh_ref[...]
    h2_ref[...] = h2.reshape(h2_ref.shape)
    lo_ref[...] = jnp.dot(h2, wr_ref[...], preferred_element_type=F32,
                          precision=lax.Precision.HIGHEST) + br_ref[...]


def _post_mixer(x, y, mod5, slab, l, ln_g, ln_b, wr, br, *, alpha):
    T, D = x.shape
    tm = slab.tile(256)
    nr = wr.shape[2]
    row = pl.BlockSpec((tm, D), lambda i: (i, 0))
    vec = pl.BlockSpec((None, 1, D), lambda i: (l, 0, 0))
    return pl.pallas_call(
        functools.partial(_post_mixer_kernel, alpha=alpha),
        grid=(T // tm,),
        in_specs=[row, row, slab.mod_spec(tm, l, 2, D), slab.mod_spec(tm, l, 4, D), slab.mod_spec(tm, l, 3, D),
                  vec, vec,
                  pl.BlockSpec((None, D, nr), lambda i: (l, 0, 0)),
                  pl.BlockSpec((None, 1, nr), lambda i: (l, 0, 0))],
        out_specs=[row, pl.BlockSpec((tm, 1, D), lambda i: (i, 0, 0)), pl.BlockSpec((tm, nr), lambda i: (i, 0))],
        out_shape=[jax.ShapeDtypeStruct((T, D), F32), jax.ShapeDtypeStruct((T, 1, D), F32),
                   jax.ShapeDtypeStruct((T, nr), F32)],
        compiler_params=_cparams(1),
    )(x, y, mod5, mod5, mod5, ln_g, ln_b, wr, br)


def _issue_rows(copy_fn, tm):
    def f(i, c):
        for q in range(ROW_SEMS):
            for cp in copy_fn(i * ROW_SEMS + q, q):
                cp.start(priority=q % 2)
        return c
    lax.fori_loop(0, tm // ROW_SEMS, f, 0, unroll=DMA_UNROLL // ROW_SEMS)


def _wait_rows(src_hbm, dst, sems):
    n = dst.shape[0] // ROW_SEMS
    for q in range(ROW_SEMS):
        pltpu.make_async_copy(src_hbm.at[pl.ds(0, n)], dst.at[pl.ds(0, n)], sems.at[q]).wait()


def _expert_changed(blk_e_ref, b):
    return jnp.logical_or(b == 0, blk_e_ref[b] != blk_e_ref[jnp.maximum(b - 1, 0)])


def _moe_ffn_kernel(blk_e_ref, nxt_e_ref, nused_ref, tok_ref, x_hbm, w1_hbm, w3_hbm, w2_hbm, o_ref,
                    rows_ref, dense_ref, s1_ref, s3_ref, s2_ref, b1_ref, b3_ref, b2_ref, sem, row_sem, *, l, tm):
    b = pl.program_id(0)
    nused = nused_ref[0]
    used = b < nused
    streams = ((w1_hbm, s1_ref, b1_ref), (w3_hbm, s3_ref, b3_ref), (w2_hbm, s2_ref, b2_ref))

    def issue_rows(blk, slot):
        def copy_fn(r, q):
            t = tok_ref[blk * tm + r]
            return (pltpu.make_async_copy(x_hbm.at[pl.ds(t, 1)], rows_ref.at[slot, pl.ds(r, 1)],
                                          row_sem.at[slot, q]),)
        _issue_rows(copy_fn, tm)

    @pl.when(jnp.logical_and(b == 0, nused > 0))
    def _():
        issue_rows(0, 0)

    @pl.when(b + 1 < nused)
    def _():
        issue_rows(b + 1, (b + 1) % 2)

    def fetch(k, e):
        w_hbm, stage_ref, _ = streams[k]
        return pltpu.make_async_copy(w_hbm.at[l, e], stage_ref, sem.at[k])

    @pl.when(jnp.logical_and(used, _expert_changed(blk_e_ref, b)))
    def _():
        e = blk_e_ref[b]
        nxt = nxt_e_ref[b]

        @pl.when(b == 0)
        def _():
            for k in range(3):
                fetch(k, e).start()

        for k in range(3):
            _, stage_ref, cast_ref = streams[k]
            fetch(k, e).wait()
            cast_ref[...] = stage_ref[...].astype(BF16)

            @pl.when(nxt >= 0)
            def _(k=k):
                fetch(k, nxt).start()

    @pl.when(used)
    def _():
        _wait_rows(x_hbm, rows_ref.at[b % 2], row_sem.at[b % 2])
        dense_ref[...] = rows_ref[b % 2].reshape(dense_ref.shape)
        xs = dense_ref[...].astype(BF16)
        h1 = jnp.dot(xs, b1_ref[...], preferred_element_type=F32)
        h3 = jnp.dot(xs, b3_ref[...], preferred_element_type=F32)
        hid = (_silu(h1) * h3).astype(BF16)
        o_ref[...] = jnp.dot(hid, b2_ref[...], preferred_element_type=F32)

    @pl.when(jnp.logical_not(used))
    def _():
        o_ref[...] = jnp.zeros_like(o_ref)


def _moe(h2, blk_e, nxt_e, nused, buf_tok, w1, w3, w2, l, *, tm):
    D, FF = w1.shape[2:]
    P = buf_tok.shape[0]
    grid_spec = pltpu.PrefetchScalarGridSpec(
        num_scalar_prefetch=4,
        grid=(P // tm,),
        in_specs=[pl.BlockSpec(memory_space=pl.ANY)] * 4,
        out_specs=pl.BlockSpec((tm, D), lambda b, *_: (b, 0)),
        scratch_shapes=[pltpu.VMEM((2, tm, 1, D), F32), pltpu.VMEM((tm, D), F32),
                        pltpu.VMEM((D, FF), F32), pltpu.VMEM((D, FF), F32), pltpu.VMEM((FF, D), F32),
                        pltpu.VMEM((D, FF), BF16), pltpu.VMEM((D, FF), BF16), pltpu.VMEM((FF, D), BF16),
                        pltpu.SemaphoreType.DMA((3,)), pltpu.SemaphoreType.DMA((2, ROW_SEMS))],
    )
    return pl.pallas_call(
        functools.partial(_moe_ffn_kernel, l=l, tm=tm),
        grid_spec=grid_spec,
        out_shape=jax.ShapeDtypeStruct((P, D), F32),
        compiler_params=_cparams(1),
    )(blk_e, nxt_e, nused, buf_tok, h2, w1, w3, w2)


def _post_moe_kernel(p0_ref, p1_ref, es_hbm, x_ref, gw_ref, g2_ref, lg_ref, lb_ref, *rest, tm, alpha, has_next):
    if has_next:
        sc_ref, sh_ref, x2_ref, h_ref, buf_ref, sem = rest
    else:
        x2_ref, buf_ref, sem = rest
    i = pl.program_id(0)
    n = pl.num_programs(0)
    p_refs = (p0_ref, p1_ref)

    def issue(tile, slot):
        def copy_fn(r, q):
            return tuple(pltpu.make_async_copy(es_hbm.at[pl.ds(p_refs[j][tile * tm + r], 1)],
                                               buf_ref.at[slot, j, pl.ds(r, 1)], sem.at[slot, j, q])
                         for j in range(TOP_K))
        _issue_rows(copy_fn, tm)

    @pl.when(i == 0)
    def _():
        issue(0, 0)

    @pl.when(i + 1 < n)
    def _():
        issue(i + 1, (i + 1) % 2)

    slot = i % 2
    for j in range(TOP_K):
        _wait_rows(es_hbm, buf_ref.at[slot, j], sem.at[slot, j])
    y = gw_ref[:, 0:1] * buf_ref[slot, 0] + gw_ref[:, 1:2] * buf_ref[slot, 1]
    x2 =_layer_norm_rows(alpha * x_ref[...] + g2_ref[...] * y, lg_ref[...], lb_ref[...])
    x2_ref[...] = x2
    if has_next:
        h_ref[...] = (x2 * (1.0 + sc_ref[...]) + sh_ref[...]).astype(h_ref.dtype)


def _post_moe(p0, p1, es, x1, gate, mod5, slab, l, ln_g, ln_b, *, alpha, has_next):
    T, D = x1.shape
    tm = slab.tile(256)
    row = pl.BlockSpec((tm, D), lambda i, *_: (i, 0))
    vec = pl.BlockSpec((None, 1, D), lambda i, *_: (l, 0, 0))
    in_specs = [pl.BlockSpec(memory_space=pl.ANY), row, pl.BlockSpec((tm, TOP_K), lambda i, *_: (i, 0)),
                slab.mod_spec(tm, l, 5, D), vec, vec]
    args = [es, x1, gate, mod5, ln_g, ln_b]
    out_specs = [row]
    out_shape = [jax.ShapeDtypeStruct((T, D), F32)]
    if has_next:
        in_specs += [slab.mod_spec(tm, l + 1, 1, D), slab.mod_spec(tm, l + 1, 0, D)]
        args += [mod5, mod5]
        out_specs.append(row)
        out_shape.append(jax.ShapeDtypeStruct((T, D), BF16))
    grid_spec = pltpu.PrefetchScalarGridSpec(
        num_scalar_prefetch=2,
        grid=(T // tm,),
        in_specs=in_specs,
        out_specs=out_specs,
        scratch_shapes=[pltpu.VMEM((2, TOP_K, tm, D), F32), pltpu.SemaphoreType.DMA((2, TOP_K, ROW_SEMS))],
    )
    res = pl.pallas_call(
        functools.partial(_post_moe_kernel, tm=tm, alpha=alpha, has_next=has_next),
        grid_spec=grid_spec,
        out_shape=out_shape,
        compiler_params=_cparams(1),
    )(p0, p1, *args)
    return (res[0], res[1]) if has_next else (res[0], None)


def _route(logits, n_groups, epg, tm):
    T = logits.shape[0]
    n_exp = n_groups * epg
    pg = jax.nn.softmax(logits[:, :n_groups], axis=-1)
    g_idx = jnp.argmax(pg, axis=-1)
    p_sel = jnp.take_along_axis(pg, g_idx[:, None], axis=-1)
    le = logits[:, n_groups:n_groups + n_exp].reshape(T, n_groups, epg)
    le = jnp.take_along_axis(le, g_idx[:, None, None], axis=1)[:, 0]
    pe = jax.nn.softmax(le, axis=-1)
    top_p, top_i = lax.top_k(pe, TOP_K)
    gate = p_sel * top_p / jnp.sum(top_p, axis=-1, keepdims=True)
    expert = (g_idx[:, None] * epg + top_i).astype(jnp.int32)

    A = T * TOP_K
    e_flat = expert.reshape(A)
    tok_flat = jnp.repeat(jnp.arange(T, dtype=jnp.int32), TOP_K)
    onehot = (e_flat[:, None] == jnp.arange(n_exp, dtype=jnp.int32)[None, :]).astype(jnp.int32)
    csum = jnp.cumsum(onehot, axis=0)
    counts = csum[-1]
    rank = jnp.take_along_axis(csum, e_flat[:, None], axis=1)[:, 0] - 1
    padded = (counts + tm - 1) // tm * tm
    pends = jnp.cumsum(padded)
    pstarts = pends - padded
    pos = (pstarts[e_flat] + rank).astype(jnp.int32)
    n_blocks = -(-A // tm) + n_exp
    P = n_blocks * tm
    buf_tok = jnp.zeros((P,), jnp.int32).at[pos].set(tok_flat)
    blk_e = jnp.clip(jnp.searchsorted(pends, jnp.arange(n_blocks, dtype=pends.dtype) * tm, side='right'),
                     0, n_exp - 1).astype(jnp.int32)
    nused = (pends[-1] // tm).astype(jnp.int32).reshape(1)
    ids = jnp.arange(n_exp, dtype=jnp.int32)
    own = jnp.where(counts > 0, ids, n_exp)
    later = lax.cummin(jnp.concatenate([own[1:], jnp.full((1,), n_exp, jnp.int32)]), reverse=True)
    nxt_e = jnp.where(later < n_exp, later, -1)[blk_e].astype(jnp.int32)
    pos2 = pos.reshape(T, TOP_K)
    return blk_e, nxt_e, nused, buf_tok, gate.astype(F32), pos2[:, 0], pos2[:, 1]


def _rope_tables(n_tokens, dk):
    freqs = dk // 4
    rows = n_tokens // GRID_W
    row = jnp.repeat(jnp.arange(rows, dtype=F32), GRID_W)
    col = jnp.tile(jnp.arange(GRID_W, dtype=F32), rows)
    inv = 1.0 / (ROPE_THETA ** (jnp.arange(freqs, dtype=F32) / freqs))
    ar, ac = row[:, None] * inv, col[:, None] * inv
    z = jnp.zeros_like(ar)
    cos = jnp.concatenate([jnp.cos(ar), jnp.cos(ar), jnp.cos(ac), jnp.cos(ac)], axis=1)
    s_lo = jnp.concatenate([-jnp.sin(ar), z, -jnp.sin(ac), z], axis=1)
    s_hi = jnp.concatenate([z, jnp.sin(ar), z, jnp.sin(ac)], axis=1)
    return cos, s_lo, s_hi


def kernel(x_prompt, x_sample, cache_k, cache_v, state_gla, c, c_ctx, w_mod, b_mod, w_in, b_gate, da_lambda,
           da_norm_w, gla_a2, gla_a_bias, gla_norm_w, w_da_proj, w_gla_proj, w_out, ln1_g, ln1_b, ln2_g, ln2_b,
           router_g_w, router_g_b, router_e_w, router_e_b, exp_w1, exp_w3, exp_w2):
    B, S, D = x_prompt.shape
    n_dec, dec_seq, _ = x_sample.shape
    depth = w_mod.shape[0]
    past, da_heads, _, dk = cache_k.shape[2:]
    dv = cache_v.shape[-1]
    gla_heads, gk, gv = state_gla.shape[3:]
    lowrank = gla_a2.shape[2]
    n_groups, epg = router_e_w.shape[2:]
    n_exp = n_groups * epg
    da_qw, da_vw = da_heads * 2 * dk, da_heads * dv
    gla_qw, gla_vw = gla_heads * gk, gla_heads * gv
    n_main = 2 * da_qw + da_vw + 2 * gla_qw + 2 * gla_vw
    assert n_main % LANE == 0 and w_in.shape[2] == n_main + 2 * lowrank + 2 * D
    alpha = (2.0 * depth) ** 0.25

    slab = _Slab(B * S, dec_seq, n_dec)
    t_ctx = slab.t_ctx
    moe_tm = 128

    x = jnp.concatenate([x_prompt.reshape(t_ctx, D), x_sample.reshape(n_dec * dec_seq, D)], axis=0)
    n_cond = 8
    cond = jnp.zeros((n_cond, D), F32).at[0].set(c_ctx).at[1:1 + n_dec].set(c)
    w_in_t = jnp.swapaxes(w_in, 1, 2)
    b_gate4 = b_gate.reshape(depth, 2, 1, D)
    nr =-(-(n_groups + n_exp) // LANE) * LANE
    wr = jnp.concatenate([router_g_w, router_e_w.reshape(depth, D, n_exp),
                          jnp.zeros((depth, D, nr - n_groups - n_exp), F32)], axis=2)
    br = jnp.concatenate([router_g_b, router_e_b.reshape(depth, n_exp),
                          jnp.zeros((depth, nr - n_groups - n_exp), F32)], axis=1).reshape(depth, 1, nr)
    a2p = jnp.zeros((depth, 2, 2 * lowrank, gla_qw), F32)
    a2p = a2p.at[:, 0, :lowrank].set(gla_a2[:, 0]).at[:, 1, lowrank:].set(gla_a2[:, 1])
    ab = gla_a_bias.reshape(depth, 2, 1, gla_qw)
    lv = da_lambda.astype(F32)
    lam_init = [0.8 - 0.6 * math.exp(-0.3 * l) for l in range(depth)]
    lam = [(jnp.exp(jnp.sum(lv[l, 0] * lv[l, 1])) - jnp.exp(jnp.sum(lv[l, 2] * lv[l, 3])) + lam_init[l]).reshape(1, 1)
           for l in range(depth)]
    rope = _rope_tables(dec_seq, dk)
    cache_k2 = cache_k.reshape(n_dec, depth, past, da_qw)
    cache_v2 = cache_v.reshape(n_dec, depth, past, da_vw)
    ln1_g3, ln1_b3 = ln1_g.reshape(depth, 1, D), ln1_b.reshape(depth, 1, D)
    ln2_g3, ln2_b3 = ln2_g.reshape(depth, 1, D), ln2_b.reshape(depth, 1, D)

    mod = _modulation(cond, w_mod, b_mod)
    mod5 = mod.reshape(depth, n_cond, 6, 1, D)

    h = _premod(x, mod5, slab, 0)
    ks, vs, sts = [], [], []
    for l in range(depth):
        u = _matmul_t(h, w_in_t, l, 0, n_main)
        ag = _matmul_t(h, w_in_t, l, n_main, LANE, tn=LANE)
        gates = _matmul_t(h, w_in_t, l, n_main + 2 * lowrank, 2 * D)
        ks.append(u[:t_ctx, da_qw:2 * da_qw].reshape(B, S, da_qw))
        vs.append(u[:t_ctx, 2 * da_qw:2 * da_qw + da_vw].reshape(B, S, da_vw))

        post = 1.0 - lam_init[l]
        nw_d = da_norm_w[l].reshape(1, dv)
        od_c = _attn_ctx(u, lam[l], nw_d, n_seq=B, L=S, heads=da_heads, dk=dk, dv=dv, post=post)
        od_d = _attn_dec(u, t_ctx, cache_k2, cache_v2, l, rope, lam[l], nw_d,
                         n_seq=n_dec, L=dec_seq, heads=da_heads, dk=dk, dv=dv, post=post)

        qcol = 2 * da_qw + da_vw
        vcol = qcol + 2 * gla_qw
        rcol = vcol + gla_vw
        nw_g = gla_norm_w[l].reshape(1, gv)
        gla_kw = dict(heads=gla_heads, K=gk, V=gv, qcol=qcol, vcol=vcol, rcol=rcol)
        og_c, st = _gla(u, ag, 0, a2p[l], ab[l], nw_g, None, n_seq=B, L=S, want_state=True, **gla_kw)
        og_d, _ = _gla(u, ag, t_ctx, a2p[l], ab[l], nw_g, state_gla[:, l], n_seq=n_dec, L=dec_seq,
                       want_state=False, **gla_kw)
        sts.append(st)

        m = _merge(od_c, od_d, og_c, og_d, w_da_proj, w_gla_proj, gates, b_gate4, l)
        y = _matmul(m, w_out, l, 0, D)
        x1, h2, logits = _post_mixer(x, y, mod5, slab, l, ln1_g3, ln1_b3, wr, br, alpha=alpha)

        blk_e, nxt_e, nused, buf_tok, gate, p0, p1 = _route(logits, n_groups, epg, moe_tm)
        es = _moe(h2, blk_e, nxt_e, nused, buf_tok, exp_w1, exp_w3, exp_w2, l, tm=moe_tm)
        x, h = _post_moe(p0, p1, es, x1, gate, mod5, slab, l, ln2_g3, ln2_b3, alpha=alpha, has_next=l + 1 < depth)

    y_prompt = x[:t_ctx].reshape(B, S, D)
    y_sample = x[t_ctx:].reshape(n_dec, dec_seq, D)
    new_k = jnp.stack(ks, axis=1).reshape(B, depth, S, da_heads, 2, dk)
    new_v = jnp.stack(vs, axis=1).reshape(B, depth, S, da_heads, dv)
    return (y_prompt, y_sample, new_k, new_v, jnp.stack(sts, axis=1))
```

```python
import functools
import math

import jax
import jax.numpy as jnp
from jax import lax
from jax.experimental import pallas as pl
from jax.experimental.pallas import tpu as pltpu

F32 = jnp.float32
BF16 = jnp.bfloat16

GRID_W = 64
ROPE_THETA = 10000.0
GLA_TAU = 16.0
GLA_CHUNK = 64
TOP_K = 2
LN_EPS = 1e-5
LANE = 128
VMEM_LIMIT = 56 * 1024 * 1024
DMA_UNROLL = 8
ROW_SEMS = 4


def _cparams(n_axes):
    return pltpu.CompilerParams(dimension_semantics=("arbitrary",) * n_axes, vmem_limit_bytes=VMEM_LIMIT)


def _tile(n, pref):
    t = min(n, pref)
    while n % t:
        t //= 2
    return t


def _silu(x):
    return x * (1.0 / (1.0 + jnp.exp(-x)))


def _sigmoid(x):
    return 1.0 / (1.0 + jnp.exp(-x))


def _mm_kernel(x_ref, w_ref, o_ref, wb_ref):
    @pl.when(pl.program_id(1) == 0)
    def _():
        wb_ref[...] = w_ref[...].astype(BF16)

    o_ref[...] = jnp.dot(x_ref[...], wb_ref[...], preferred_element_type=F32).astype(o_ref.dtype)


def _matmul(x, w, l, col0, ncols, *, tm=1024, tn=512, out_dtype=F32):
    M, K = x.shape
    tm = _tile(M, tm)
    tn = _tile(ncols, tn)
    assert col0 % tn == 0
    cb0 = col0 // tn
    return pl.pallas_call(
        _mm_kernel,
        grid=(ncols // tn, M // tm),
        in_specs=[
            pl.BlockSpec((tm, K), lambda j, i: (i, 0)),
            pl.BlockSpec((None, K, tn), lambda j, i: (l, 0, cb0 + j)),
        ],
        out_specs=pl.BlockSpec((tm, tn), lambda j, i: (i, j)),
        out_shape=jax.ShapeDtypeStruct((M, ncols), out_dtype),
        scratch_shapes=[pltpu.VMEM((K, tn), BF16)],
        compiler_params=_cparams(2),
    )(x, w)


_NT = (((1,), (1,)), ((), ()))


def _mm_t_kernel(x_ref, w_ref, o_ref, wb_ref):
    @pl.when(pl.program_id(1) == 0)
    def _():
        wb_ref[...] = w_ref[...].astype(BF16)

    o_ref[...] = lax.dot_general(x_ref[...], wb_ref[...], _NT, preferred_element_type=F32).astype(o_ref.dtype)


def _mm_t_shift_kernel(x_ref, wa_ref, wt_ref, o_ref, wb_ref, *, shift):
    @pl.when(pl.program_id(1) == 0)
    def _():
        tn = wb_ref.shape[0]
        wb_ref[0:tn - shift, :] = wa_ref[shift:, :].astype(BF16)
        wb_ref[tn - shift:, :] = wt_ref[:shift, :].astype(BF16)

    o_ref[...] = lax.dot_general(x_ref[...], wb_ref[...], _NT, preferred_element_type=F32).astype(o_ref.dtype)


def _mm_t_copies_kernel(x_ref, w_ref, o_ref, *rest, n_rows, windows):
    copy_refs, wb_ref = rest[:-1], rest[-1]
    j, i = pl.program_id(0), pl.program_id(1)

    @pl.when(i == 0)
    def _():
        wb_ref[...] = w_ref[...].astype(BF16)

    acc = lax.dot_general(x_ref[...], wb_ref[...], _NT, preferred_element_type=F32)
    o_ref[...] = acc
    for c_ref, (j0, j1) in zip(copy_refs, windows):
        @pl.when(jnp.logical_and(i < n_rows, jnp.logical_and(j >= j0, j < j1)))
        def _(c_ref=c_ref):
            c_ref[...] = acc


def _matmul_t_with_copies(x, wt, l, nrows, copy_rows, copy_cols, *, tm=1024, tn=512):
    M, K = x.shape
    tm = _tile(math.gcd(M, copy_rows), tm)
    tn = _tile(nrows, tn)
    n_rows = copy_rows // tm
    windows = []
    for c0, c1 in copy_cols:
        assert c0 % tn == 0 and c1 % tn == 0
        windows.append((c0 // tn, c1 // tn))

    def copy_spec(j0, j1):
        def imap(j, i):
            inside = jnp.logical_and(j >= j0, j < j1)
            row = jnp.where(j < j0, 0, jnp.where(inside, jnp.minimum(i, n_rows - 1), n_rows - 1))
            col = jnp.where(j < j0, 0, jnp.where(inside, j - j0, j1 - j0 - 1))
            return (row, col)
        return pl.BlockSpec((tm, tn), imap)

    return pl.pallas_call(
        functools.partial(_mm_t_copies_kernel, n_rows=n_rows, windows=tuple(windows)),
        grid=(nrows // tn, M // tm),
        in_specs=[
            pl.BlockSpec((tm, K), lambda j, i: (i, 0)),
            pl.BlockSpec((None, tn, K), lambda j, i: (l, j, 0)),
        ],
        out_specs=[pl.BlockSpec((tm, tn), lambda j, i: (i, j))] + [copy_spec(j0, j1) for j0, j1 in windows],
        out_shape=[jax.ShapeDtypeStruct((M, nrows), F32)]
        + [jax.ShapeDtypeStruct((copy_rows, c1 - c0), F32) for c0, c1 in copy_cols],
        scratch_shapes=[pltpu.VMEM((tn, K), BF16)],
        compiler_params=_cparams(2),
    )(x, wt)


def _matmul_t(x, wt, l, row0, nrows, *, tm=1024, tn=512, out_dtype=F32):
    M, K = x.shape
    tm = _tile(M, tm)
    tn = _tile(nrows, tn)
    base = row0 // LANE * LANE
    shift = row0 - base
    assert base % tn == 0 and tn % LANE == 0 and shift % 32 == 0
    rb0 = base // tn
    in_specs = [
        pl.BlockSpec((tm, K), lambda j, i: (i, 0)),
        pl.BlockSpec((None, tn, K), lambda j, i: (l, rb0 + j, 0)),
    ]
    args = [x, wt]
    kern = _mm_t_kernel
    if shift:
        lpt = tn // LANE
        in_specs.append(pl.BlockSpec((None, LANE, K), lambda j, i: (l, (rb0 + j + 1) * lpt, 0)))
        args.append(wt)
        kern = functools.partial(_mm_t_shift_kernel, shift=shift)
    return pl.pallas_call(
        kern,
        grid=(nrows // tn, M // tm),
        in_specs=in_specs,
        out_specs=pl.BlockSpec((tm, tn), lambda j, i: (i, j)),
        out_shape=jax.ShapeDtypeStruct((M, nrows), out_dtype),
        scratch_shapes=[pltpu.VMEM((tn, K), BF16)],
        compiler_params=_cparams(2),
    )(*args)


def _mod_kernel(c_ref, w_ref, b_ref, o_ref):
    a = _silu(c_ref[...]).astype(BF16)
    o_ref[...] = jnp.dot(a, w_ref[...].astype(BF16), preferred_element_type=F32) + b_ref[...]


def _modulation(cond, w_mod, b_mod, *, tn=512):
    depth, D, N = w_mod.shape
    R = cond.shape[0]
    tn = _tile(N, tn)
    return pl.pallas_call(
        _mod_kernel,
        grid=(depth, N // tn),
        in_specs=[
            pl.BlockSpec((R, D), lambda l, j: (0, 0)),
            pl.BlockSpec((None, D, tn), lambda l, j: (l, 0, j)),
            pl.BlockSpec((None, 1, tn), lambda l, j: (l, 0, j)),
        ],
        out_specs=pl.BlockSpec((None, R, tn), lambda l, j: (l, 0, j)),
        out_shape=jax.ShapeDtypeStruct((depth, R, N), F32),
        compiler_params=_cparams(2),
    )(cond, w_mod, b_mod.reshape(depth, 1, N))


class _Slab:
    def __init__(self, t_ctx, dec_seq, n_dec):
        self.t_ctx, self.dec_seq, self.n_dec = t_ctx, dec_seq, n_dec
        self.T = t_ctx + dec_seq * n_dec

    def tile(self, pref):
        t = min(pref, self.t_ctx, self.dec_seq)
        while self.t_ctx % t or self.dec_seq % t:
            t //= 2
        return t

    def mod_spec(self, tm, l, which, D):
        n_ctx = self.t_ctx // tm
        per = self.dec_seq // tm

        def imap(i, *_):
            return (l, jnp.where(i < n_ctx, 0, (i - n_ctx) // per + 1), which, 0, 0)

        return pl.BlockSpec((None, None, None, 1, D), imap)


def _premod_kernel(x_ref, sc_ref, sh_ref, h_ref):
    h_ref[...] = (x_ref[...] * (1.0 + sc_ref[...]) + sh_ref[...]).astype(h_ref.dtype)


def _premod(x, mod5, slab, l):
    T, D = x.shape
    tm = slab.tile(256)
    return pl.pallas_call(
        _premod_kernel,
        grid=(T // tm,),
        in_specs=[pl.BlockSpec((tm, D), lambda i: (i, 0)), slab.mod_spec(tm, l, 1, D), slab.mod_spec(tm, l, 0, D)],
        out_specs=pl.BlockSpec((tm, D), lambda i: (i, 0)),
        out_shape=jax.ShapeDtypeStruct((T, D), BF16),
        compiler_params=_cparams(1),
    )(x, mod5, mod5)


def _exp_rows(s):
    e = jnp.exp(s - jnp.max(s, axis=-1, keepdims=True))
    return e, jnp.sum(e, axis=-1, keepdims=True)


def _diff_attn_rows(q, k1, k2, v, lam, scale, dk):
    nt = (((1,), (1,)), ((), ()))
    qs = q * scale
    e1, z1 = _exp_rows(lax.dot_general(qs[:, :dk].astype(BF16), k1, nt, preferred_element_type=F32))
    e2, z2 = _exp_rows(lax.dot_general(qs[:, dk:].astype(BF16), k2, nt, preferred_element_type=F32))
    o1 = jnp.dot(e1.astype(BF16), v, preferred_element_type=F32)
    o2 = jnp.dot(e2.astype(BF16), v, preferred_element_type=F32)
    return o1 * (1.0 / z1) - o2 * (lam / z2)


def _rms_rows(o, w):
    return o * lax.rsqrt(jnp.mean(o * o, axis=-1, keepdims=True) + LN_EPS) * w


def _attn_ctx_kernel(lam_ref, q_ref, k_ref, v_ref, nw_ref, o_ref, *, heads, dk, dv, post):
    lam = lam_ref[0, 0]
    scale = dk ** -0.5
    for h in range(heads):
        q = q_ref[:, h * 2 * dk:(h + 1) * 2 * dk]
        k = k_ref[:, h * 2 * dk:(h + 1) * 2 * dk].astype(BF16)
        v = v_ref[:, h * dv:(h + 1) * dv].astype(BF16)
        o = _diff_attn_rows(q, k[:, :dk], k[:, dk:], v, lam, scale, dk)
        o_ref[:, h * dv:(h + 1) * dv] = (_rms_rows(o, nw_ref[...]) * post).astype(o_ref.dtype)


def _attn_ctx(u, lam, norm_w, *, n_seq, L, heads, dk, dv, post):
    qw = heads * 2 * dk
    vw = heads * dv
    assert qw == vw
    kern = functools.partial(_attn_ctx_kernel, heads=heads, dk=dk, dv=dv, post=post)
    return pl.pallas_call(
        kern,
        grid=(n_seq,),
        in_specs=[
            pl.BlockSpec(memory_space=pltpu.SMEM),
            pl.BlockSpec((L, qw), lambda b: (b, 0)),
            pl.BlockSpec((L, qw), lambda b: (b, 1)),
            pl.BlockSpec((L, vw), lambda b: (b, 2)),
            pl.BlockSpec((1, dv), lambda b: (0, 0)),
        ],
        out_specs=pl.BlockSpec((L, vw), lambda b: (b, 0)),
        out_shape=jax.ShapeDtypeStruct((n_seq * L, vw), BF16),
        compiler_params=_cparams(1),
    )(lam, u, u, u, norm_w)


def _rope_rows(x, cos, s_lo, s_hi, dk):
    q4 = dk // 4
    return x * cos + pltpu.roll(x, dk - q4, 1) * s_lo + pltpu.roll(x, q4, 1) * s_hi


def _attn_dec_kernel(lam_ref, q_ref, k_ref, v_ref, ck_ref, cv_ref, cos_ref, slo_ref, shi_ref, nw_ref, o_ref,
                     kb_ref, vb_ref, *, dk, dv, past, L, tq, post):
    lam = lam_ref[0, 0]
    scale = dk ** -0.5
    cos, slo, shi = cos_ref[...], slo_ref[...], shi_ref[...]
    kb_ref[0:past, :] = ck_ref[...].astype(BF16)
    vb_ref[0:past, :] = cv_ref[...].astype(BF16)
    for c in range(2):
        kb_ref[past:past + L, c * dk:(c + 1) * dk] = _rope_rows(
            k_ref[:, c * dk:(c + 1) * dk], cos, slo, shi, dk).astype(BF16)
    vb_ref[past:past + L, :] = v_ref[...].astype(BF16)
    k1 = kb_ref[:, :dk]
    k2 = kb_ref[:, dk:]
    v = vb_ref[...]
    for i in range(L // tq):
        rows = slice(i * tq, (i + 1) * tq)
        q = jnp.concatenate(
            [_rope_rows(q_ref[rows, c * dk:(c + 1) * dk], cos[rows], slo[rows], shi[rows], dk) for c in range(2)],
            axis=1)
        o = _diff_attn_rows(q, k1, k2, v, lam, scale, dk)
        o_ref[rows, :] = (_rms_rows(o, nw_ref[...]) * post).astype(o_ref.dtype)


def _attn_dec(u, row0, cache_k, cache_v, l, rope, lam, norm_w, *, n_seq, L, heads, dk, dv, post):
    past = cache_k.shape[2]
    assert row0 % L == 0
    rb0 = row0 // L
    tq = _tile(L, 256)
    kern = functools.partial(_attn_dec_kernel, dk=dk, dv=dv, past=past, L=L, tq=tq, post=post)
    cos, slo, shi = rope
    tab = pl.BlockSpec((L, dk), lambda b, h: (0, 0))
    return pl.pallas_call(
        kern,
        grid=(n_seq, heads),
        in_specs=[
            pl.BlockSpec(memory_space=pltpu.SMEM),
            pl.BlockSpec((L, 2 * dk), lambda b, h: (rb0 + b, h)),
            pl.BlockSpec((L, 2 * dk), lambda b, h: (rb0 + b, heads + h)),
            pl.BlockSpec((L, dv), lambda b, h: (rb0 + b, 2 * heads + h)),
            pl.BlockSpec((None, None, past, 2 * dk), lambda b, h: (b, l, 0, h)),
            pl.BlockSpec((None, None, past, dv), lambda b, h: (b, l, 0, h)),
            tab, tab, tab,
            pl.BlockSpec((1, dv), lambda b, h: (0, 0)),
        ],
        out_specs=pl.BlockSpec((L, dv), lambda b, h: (b, h)),
        out_shape=jax.ShapeDtypeStruct((n_seq * L, heads * dv), BF16),
        scratch_shapes=[pltpu.VMEM((past + L, 2 * dk), BF16), pltpu.VMEM((past + L, dv), BF16)],
        compiler_params=_cparams(2),
    )(lam, u, u, u, cache_k, cache_v, cos, slo, shi, norm_w)


def _log_sigmoid(x):
    return -(jnp.maximum(-x, 0.0) + jnp.log1p(jnp.exp(-jnp.abs(x))))


def _split3_bf16(x):
    hi = x.astype(BF16)
    r1 = x - hi.astype(F32)
    mid = r1.astype(BF16)
    lo = (r1 - mid.astype(F32)).astype(BF16)
    return hi, mid, lo


def _gla_kernel(*refs, L, K, V, G, has_s0, want_state):
    q_ref, k_ref, v_ref, r_ref, ag_ref, a2_ref, ab_ref, nw_ref = refs[:8]
    refs = refs[8:]
    s0_ref = None
    if has_s0:
        s0_ref, refs = refs[0], refs[1:]
    o_ref, refs = refs[0], refs[1:]
    sfin_ref = None
    if want_state:
        sfin_ref, refs = refs[0], refs[1:]
    st_ref, of_ref, ob_ref = refs

    C = GLA_CHUNK
    N = L // C
    r2 = a2_ref.shape[1]
    row = lax.broadcasted_iota(jnp.int32, (C, C), 0)
    col = lax.broadcasted_iota(jnp.int32, (C, C), 1)
    nt = (((1,), (1,)), ((), ()))
    tn = (((0,), (0,)), ((), ()))
    keeps = (col <= row, col >= row)
    tris = tuple(jnp.where(kp, 1.0, 0.0).astype(BF16) for kp in keeps)

    for d in range(2):
        for g in range(G):
            st_ref[d, g] = s0_ref[d, g].T if has_s0 else jnp.zeros((V, K), F32)

    o_refs = (of_ref, ob_ref)

    def body(i, carry):
        sls, q_in, k_in, k_out, dec = [], [], [], [], []
        for d in range(2):
            n = i if d == 0 else N - 1 - i
            sl = pl.ds(pl.multiple_of(n * C, C), C)
            agc = ag_ref[sl, :][:, :r2].astype(BF16)
            logit = jnp.dot(agc, a2_ref[d].astype(BF16), preferred_element_type=F32) + ab_ref[d]
            la = _log_sigmoid(logit) * (1.0 / GLA_TAU)
            hi, mid, lo = _split3_bf16(la)
            bc = (jnp.dot(tris[d], hi, preferred_element_type=F32) + jnp.dot(tris[d], mid, preferred_element_type=F32)
                  + jnp.dot(tris[d], lo, preferred_element_type=F32))
            blast = bc[C - 1:C, :] if d == 0 else bc[0:1, :]
            kc = k_ref[sl, :]
            sls.append(sl)
            q_in.append((q_ref[sl, :] * (K ** -0.5) * jnp.exp(bc)).astype(BF16))
            k_in.append((kc * jnp.exp(-bc)).astype(BF16))
            k_out.append((kc * jnp.exp(blast - bc)).astype(BF16))
            dec.append(jnp.exp(blast))
        pairs = [(d, g) for d in range(2) for g in range(G)]
        ksl = lambda g: slice(g * K, (g + 1) * K)
        vsl = lambda g: slice(g * V, (g + 1) * V)
        att = {(d, g): lax.dot_general(q_in[d][:, ksl(g)], k_in[d][:, ksl(g)], nt, preferred_element_type=F32)
               for d, g in pairs}
        for d, g in pairs:
            o_refs[d][sls[d], vsl(g)] = lax.dot_general(q_in[d][:, ksl(g)], st_ref[d, g].astype(BF16), nt,
                                                        preferred_element_type=F32)
        for d, g in pairs:
            vc = v_ref[sls[d], vsl(g)].astype(BF16)
            ds_t = lax.dot_general(vc, k_out[d][:, ksl(g)], tn, preferred_element_type=F32)
            st_ref[d, g] = dec[d][:, ksl(g)] * st_ref[d, g] + ds_t
        for d, g in pairs:
            vc = v_ref[sls[d], vsl(g)].astype(BF16)
            a = jnp.where(keeps[d], att[d, g], 0.0).astype(BF16)
            o_refs[d][sls[d], vsl(g)] += jnp.dot(a, vc, preferred_element_type=F32)
        return carry

    lax.fori_loop(0, N, body, 0)
    if want_state:
        for d in range(2):
            for g in range(G):
                sfin_ref[d, g] = st_ref[d, g].T
    for g in range(G):
        vs = slice(g * V, (g + 1) * V)
        og = _rms_rows(of_ref[:, vs] + ob_ref[:, vs], nw_ref[...]) * _silu(r_ref[:, vs])
        o_ref[:, vs] = og.astype(o_ref.dtype)


def _gla(u, ag, row0, a2p, ab, norm_w, s0, *, n_seq, L, heads, K, V, qcol, vcol, rcol, want_state):
    G = _tile(heads, 4)
    gk, gv = G * K, G * V
    assert row0 % L == 0 and qcol % gk == 0 and (heads * K) % gk == 0 and vcol % gv == 0 and rcol % gv == 0
    rb0 = row0 // L
    qb, kb, vb, rb = qcol // gk, (qcol + heads * K) // gk, vcol // gv, rcol // gv
    r2 = a2p.shape[1]
    has_s0 = s0 is not None
    kern = functools.partial(_gla_kernel, L=L, K=K, V=V, G=G, has_s0=has_s0, want_state=want_state)
    in_specs = [
        pl.BlockSpec((L, gk), lambda b, h: (rb0 + b, qb + h)),
        pl.BlockSpec((L, gk), lambda b, h: (rb0 + b, kb + h)),
        pl.BlockSpec((L, gv), lambda b, h: (rb0 + b, vb + h)),
        pl.BlockSpec((L, gv), lambda b, h: (rb0 + b, rb + h)),
        pl.BlockSpec((L, LANE), lambda b, h: (rb0 + b, 0)),
        pl.BlockSpec((2, r2, gk), lambda b, h: (0, 0, h)),
        pl.BlockSpec((2, 1, gk), lambda b, h: (0, 0, h)),
        pl.BlockSpec((1, V), lambda b, h: (0, 0)),
    ]
    args = [u, u, u, u, ag, a2p, ab, norm_w]
    if has_s0:
        in_specs.append(pl.BlockSpec((None, 2, G, K, V), lambda b, h: (b, 0, h, 0, 0)))
        args.append(s0)
    out_specs = [pl.BlockSpec((L, gv), lambda b, h: (b, h))]
    out_shape = [jax.ShapeDtypeStruct((n_seq * L, heads * V), BF16)]
    if want_state:
        out_specs.append(pl.BlockSpec((None, 2, G, K, V), lambda b, h: (b, 0, h, 0, 0)))
        out_shape.append(jax.ShapeDtypeStruct((n_seq, 2, heads, K, V), F32))
    res = pl.pallas_call(
        kern,
        grid=(n_seq, heads // G),
        in_specs=in_specs,
        out_specs=out_specs,
        out_shape=out_shape,
        scratch_shapes=[pltpu.VMEM((2, G, V, K), F32), pltpu.VMEM((L, gv), F32), pltpu.VMEM((L, gv), F32)],
        compiler_params=_cparams(2),
    )(*args)
    return res if want_state else (res[0], None)


def _merge_kernel(odc_ref, odd_ref, ogc_ref, ogd_ref, wd_ref, wg_ref, gd_ref, gg_ref, bd_ref, bg_ref, o_ref,
                  wdb_ref, wgb_ref, *, n_ctx):
    i = pl.program_id(1)

    @pl.when(i == 0)
    def _():
        wdb_ref[...] = wd_ref[...].astype(BF16)
        wgb_ref[...] = wg_ref[...].astype(BF16)

    def run(od_ref, og_ref):
        yd = jnp.dot(od_ref[...], wdb_ref[...], preferred_element_type=F32)
        yg = jnp.dot(og_ref[...], wgb_ref[...], preferred_element_type=F32)
        m = _sigmoid(gd_ref[...] + bd_ref[...]) * yd + _sigmoid(gg_ref[...] + bg_ref[...]) * yg
        o_ref[...] = m.astype(o_ref.dtype)

    @pl.when(i < n_ctx)
    def _():
        run(odc_ref, ogc_ref)

    @pl.when(i >= n_ctx)
    def _():
        run(odd_ref, ogd_ref)


def _merge(od_c, od_d, og_c, og_d, w_da, w_gla, gates, b_gate4, l, *, tm=512, tn=512):
    t_ctx, Kd = od_c.shape
    t_dec = od_d.shape[0]
    Kg = og_c.shape[1]
    D = w_da.shape[2]
    tm = _tile(math.gcd(t_ctx, t_dec), tm)
    tn = _tile(D, tn)
    nj = D // tn
    n_ctx = t_ctx // tm
    T = t_ctx + t_dec
    ctx_rows = lambda j, i: (jnp.minimum(i, n_ctx - 1), 0)
    dec_rows = lambda j, i: (jnp.maximum(i - n_ctx, 0), 0)
    return pl.pallas_call(
        functools.partial(_merge_kernel, n_ctx=n_ctx),
        grid=(nj, T // tm),
        in_specs=[
            pl.BlockSpec((tm, Kd), ctx_rows),
            pl.BlockSpec((tm, Kd), dec_rows),
            pl.BlockSpec((tm, Kg), ctx_rows),
            pl.BlockSpec((tm, Kg), dec_rows),
            pl.BlockSpec((None, Kd, tn), lambda j, i: (l, 0, j)),
            pl.BlockSpec((None, Kg, tn), lambda j, i: (l, 0, j)),
            pl.BlockSpec((tm, tn), lambda j, i: (i, j)),
            pl.BlockSpec((tm, tn), lambda j, i: (i, nj + j)),
            pl.BlockSpec((None, None, 1, tn), lambda j, i: (l, 0, 0, j)),
            pl.BlockSpec((None, None, 1, tn), lambda j, i: (l, 1, 0, j)),
        ],
        out_specs=pl.BlockSpec((tm, tn), lambda j, i: (i, j)),
        out_shape=jax.ShapeDtypeStruct((T, D), BF16),
        scratch_shapes=[pltpu.VMEM((Kd, tn), BF16), pltpu.VMEM((Kg, tn), BF16)],
        compiler_params=_cparams(2),
    )(od_c, od_d, og_c, og_d, w_da, w_gla, gates, gates, b_gate4, b_gate4)


def _layer_norm_rows(z, g, b):
    mu = jnp.mean(z, axis=-1, keepdims=True)
    zc = z - mu
    var = jnp.mean(zc * zc, axis=-1, keepdims=True)
    return zc * lax.rsqrt(var + LN_EPS) * g + b


def _post_mixer_kernel(x_ref, y_ref, g1_ref, sc_ref, sh_ref, lg_ref, lb_ref, wr_ref, br_ref,
                       x1_ref, h2_ref, lo_ref, *, alpha):
    x1 = _layer_norm_rows(alpha * x_ref[...] + g1_ref[...] * y_ref[...], lg_ref[...], lb_ref[...])
    x1_ref[...] = x1
    h2 = x1 * (1.0 + sc_ref[...]) + sh_ref[...]
    h2_ref[...] = h2.reshape(h2_ref.shape)
    lo_ref[...] = jnp.dot(h2, wr_ref[...], preferred_element_type=F32,
                          precision=lax.Precision.HIGHEST) + br_ref[...]


def _post_mixer(x, y, mod5, slab, l, ln_g, ln_b, wr, br, *, alpha):
    T, D = x.shape
    tm = slab.tile(256)
    nr = wr.shape[2]
    row = pl.BlockSpec((tm, D), lambda i: (i, 0))
    vec = pl.BlockSpec((None, 1, D), lambda i: (l, 0, 0))
    return pl.pallas_call(
        functools.partial(_post_mixer_kernel, alpha=alpha),
        grid=(T // tm,),
        in_specs=[row, row, slab.mod_spec(tm, l, 2, D), slab.mod_spec(tm, l, 4, D), slab.mod_spec(tm, l, 3, D),
                  vec, vec,
                  pl.BlockSpec((None, D, nr), lambda i: (l, 0, 0)),
                  pl.BlockSpec((None, 1, nr), lambda i: (l, 0, 0))],
        out_specs=[row, pl.BlockSpec((tm, 1, D), lambda i: (i, 0, 0)), pl.BlockSpec((tm, nr), lambda i: (i, 0))],
        out_shape=[jax.ShapeDtypeStruct((T, D), F32), jax.ShapeDtypeStruct((T, 1, D), F32),
                   jax.ShapeDtypeStruct((T, nr), F32)],
        compiler_params=_cparams(1),
    )(x, y, mod5, mod5, mod5, ln_g, ln_b, wr, br)


def _issue_rows(copy_fn, tm, priority=None):
    def f(i, c):
        for q in range(ROW_SEMS):
            for cp in copy_fn(i * ROW_SEMS + q, q):
                cp.start(priority=q % 2 if priority is None else priority)
        return c
    lax.fori_loop(0, tm // ROW_SEMS, f, 0, unroll=DMA_UNROLL // ROW_SEMS)


def _wait_rows(src_hbm, dst, sems):
    n = dst.shape[0] // ROW_SEMS
    for q in range(ROW_SEMS):
        pltpu.make_async_copy(src_hbm.at[pl.ds(0, n)], dst.at[pl.ds(0, n)], sems.at[q]).wait()


def _expert_changed(blk_e_ref, b):
    return jnp.logical_or(b == 0, blk_e_ref[b] != blk_e_ref[jnp.maximum(b - 1, 0)])


def _moe_ffn_kernel(blk_e_ref, nxt_e_ref, nused_ref, tok_ref, x_hbm, w1_hbm, w3_hbm, w2_hbm, o_ref,
                    rows_ref, dense_ref, s1_ref, s3_ref, s2_ref, b1_ref, b3_ref, b2_ref, sem, row_sem, *, l, tm):
    b = pl.program_id(0)
    nused = nused_ref[0]
    used = b < nused
    streams = ((w1_hbm, s1_ref, b1_ref), (w3_hbm, s3_ref, b3_ref), (w2_hbm, s2_ref, b2_ref))

    def issue_rows(blk, slot):
        def copy_fn(r, q):
            t = tok_ref[blk * tm + r]
            return (pltpu.make_async_copy(x_hbm.at[pl.ds(t, 1)], rows_ref.at[slot, pl.ds(r, 1)],
                                          row_sem.at[slot, q]),)
        _issue_rows(copy_fn, tm, priority=1)

    @pl.when(jnp.logical_and(b == 0, nused > 0))
    def _():
        issue_rows(0, 0)

    @pl.when(b + 1 < nused)
    def _():
        issue_rows(b + 1, (b + 1) % 2)

    def fetch(k, e):
        w_hbm, stage_ref, _ = streams[k]
        return pltpu.make_async_copy(w_hbm.at[l, e], stage_ref, sem.at[k])

    @pl.when(jnp.logical_and(used, _expert_changed(blk_e_ref, b)))
    def _():
        e = blk_e_ref[b]
        nxt = nxt_e_ref[b]

        @pl.when(b == 0)
        def _():
            for k in range(3):
                fetch(k, e).start()

        for k in range(3):
            _, stage_ref, cast_ref = streams[k]
            fetch(k, e).wait()
            cast_ref[...] = stage_ref[...].astype(BF16)

            @pl.when(nxt >= 0)
            def _(k=k):
                fetch(k, nxt).start()

    @pl.when(used)
    def _():
        _wait_rows(x_hbm, rows_ref.at[b % 2], row_sem.at[b % 2])
        dense_ref[...] = rows_ref[b % 2].reshape(dense_ref.shape)
        xs = dense_ref[...].astype(BF16)
        h1 = jnp.dot(xs, b1_ref[...], preferred_element_type=F32)
        h3 = jnp.dot(xs, b3_ref[...], preferred_element_type=F32)
        hid = (_silu(h1) * h3).astype(BF16)
        o_ref[...] = jnp.dot(hid, b2_ref[...], preferred_element_type=F32)

    @pl.when(jnp.logical_not(used))
    def _():
        o_ref[...] = jnp.zeros_like(o_ref)


def _moe(h2, blk_e, nxt_e, nused, buf_tok, w1, w3, w2, l, *, tm):
    D, FF = w1.shape[2:]
    P = buf_tok.shape[0]
    grid_spec = pltpu.PrefetchScalarGridSpec(
        num_scalar_prefetch=4,
        grid=(P // tm,),
        in_specs=[pl.BlockSpec(memory_space=pl.ANY)] * 4,
        out_specs=pl.BlockSpec((tm, D), lambda b, *_: (b, 0)),
        scratch_shapes=[pltpu.VMEM((2, tm, 1, D), F32), pltpu.VMEM((tm, D), F32),
                        pltpu.VMEM((D, FF), F32), pltpu.VMEM((D, FF), F32), pltpu.VMEM((FF, D), F32),
                        pltpu.VMEM((D, FF), BF16), pltpu.VMEM((D, FF), BF16), pltpu.VMEM((FF, D), BF16),
                        pltpu.SemaphoreType.DMA((3,)), pltpu.SemaphoreType.DMA((2, ROW_SEMS))],
    )
    return pl.pallas_call(
        functools.partial(_moe_ffn_kernel, l=l, tm=tm),
        grid_spec=grid_spec,
        out_shape=jax.ShapeDtypeStruct((P, D), F32),
        compiler_params=_cparams(1),
    )(blk_e, nxt_e, nused, buf_tok, h2, w1, w3, w2)


def _post_moe_kernel(p0_ref, p1_ref, es_hbm, x_ref, gw_ref, g2_ref, lg_ref, lb_ref, *rest, tm, alpha, has_next):
    if has_next:
        sc_ref, sh_ref, x2_ref, h_ref, buf_ref, sem = rest
    else:
        x2_ref, buf_ref, sem = rest
    i = pl.program_id(0)
    n = pl.num_programs(0)
    p_refs = (p0_ref, p1_ref)

    def issue(tile, slot):
        def copy_fn(r, q):
            return tuple(pltpu.make_async_copy(es_hbm.at[pl.ds(p_refs[j][tile * tm + r], 1)],
                                               buf_ref.at[slot, j, pl.ds(r, 1)], sem.at[slot, j, q])
                         for j in range(TOP_K))
        _issue_rows(copy_fn, tm)

    @pl.when(i == 0)
    def _():
        issue(0, 0)

    @pl.when(i + 1 < n)
    def _():
        issue(i + 1, (i + 1) % 2)

    slot = i % 2
    for j in range(TOP_K):
        _wait_rows(es_hbm, buf_ref.at[slot, j], sem.at[slot, j])
    y = gw_ref[:, 0:1] * buf_ref[slot, 0] + gw_ref[:, 1:2] * buf_ref[slot, 1]
    x2 =_layer_norm_rows(alpha * x_ref[...] + g2_ref[...] * y, lg_ref[...], lb_ref[...])
    x2_ref[...] = x2
    if has_next:
        h_ref[...] = (x2 * (1.0 + sc_ref[...]) + sh_ref[...]).astype(h_ref.dtype)


def _post_moe(p0, p1, es, x1, gate, mod5, slab, l, ln_g, ln_b, *, alpha, has_next):
    T, D = x1.shape
    tm = slab.tile(256)
    row = pl.BlockSpec((tm, D), lambda i, *_: (i, 0))
    vec = pl.BlockSpec((None, 1, D), lambda i, *_: (l, 0, 0))
    in_specs = [pl.BlockSpec(memory_space=pl.ANY), row, pl.BlockSpec((tm, TOP_K), lambda i, *_: (i, 0)),
                slab.mod_spec(tm, l, 5, D), vec, vec]
    args = [es, x1, gate, mod5, ln_g, ln_b]
    out_specs = [row]
    out_shape = [jax.ShapeDtypeStruct((T, D), F32)]
    if has_next:
        in_specs += [slab.mod_spec(tm, l + 1, 1, D), slab.mod_spec(tm, l + 1, 0, D)]
        args += [mod5, mod5]
        out_specs.append(row)
        out_shape.append(jax.ShapeDtypeStruct((T, D), BF16))
    grid_spec = pltpu.PrefetchScalarGridSpec(
        num_scalar_prefetch=2,
        grid=(T // tm,),
        in_specs=in_specs,
        out_specs=out_specs,
        scratch_shapes=[pltpu.VMEM((2, TOP_K, tm, D), F32), pltpu.SemaphoreType.DMA((2, TOP_K, ROW_SEMS))],
    )
    res = pl.pallas_call(
        functools.partial(_post_moe_kernel, tm=tm, alpha=alpha, has_next=has_next),
        grid_spec=grid_spec,
        out_shape=out_shape,
        compiler_params=_cparams(1),
    )(p0, p1, *args)
    return (res[0], res[1]) if has_next else (res[0], None)


def _route(logits, n_groups, epg, tm):
    T = logits.shape[0]
    n_exp = n_groups * epg
    pg = jax.nn.softmax(logits[:, :n_groups], axis=-1)
    g_idx = jnp.argmax(pg, axis=-1)
    p_sel = jnp.take_along_axis(pg, g_idx[:, None], axis=-1)
    le = logits[:, n_groups:n_groups + n_exp].reshape(T, n_groups, epg)
    le = jnp.take_along_axis(le, g_idx[:, None, None], axis=1)[:, 0]
    pe = jax.nn.softmax(le, axis=-1)
    top_p, top_i = lax.top_k(pe, TOP_K)
    gate = p_sel * top_p / jnp.sum(top_p, axis=-1, keepdims=True)
    expert = (g_idx[:, None] * epg + top_i).astype(jnp.int32)

    A = T * TOP_K
    e_flat = expert.reshape(A)
    tok_flat = jnp.repeat(jnp.arange(T, dtype=jnp.int32), TOP_K)
    onehot = (e_flat[:, None] == jnp.arange(n_exp, dtype=jnp.int32)[None, :]).astype(jnp.int32)
    csum = jnp.cumsum(onehot, axis=0)
    counts = csum[-1]
    rank = jnp.take_along_axis(csum, e_flat[:, None], axis=1)[:, 0] - 1
    padded = (counts + tm - 1) // tm * tm
    pends = jnp.cumsum(padded)
    pstarts = pends - padded
    pos = (pstarts[e_flat] + rank).astype(jnp.int32)
    n_blocks = -(-A // tm) + n_exp
    P = n_blocks * tm
    buf_tok = jnp.zeros((P,), jnp.int32).at[pos].set(tok_flat)
    blk_e = jnp.clip(jnp.searchsorted(pends, jnp.arange(n_blocks, dtype=pends.dtype) * tm, side='right'),
                     0, n_exp - 1).astype(jnp.int32)
    nused = (pends[-1] // tm).astype(jnp.int32).reshape(1)
    ids = jnp.arange(n_exp, dtype=jnp.int32)
    own = jnp.where(counts > 0, ids, n_exp)
    later = lax.cummin(jnp.concatenate([own[1:], jnp.full((1,), n_exp, jnp.int32)]), reverse=True)
    nxt_e = jnp.where(later < n_exp, later, -1)[blk_e].astype(jnp.int32)
    pos2 = pos.reshape(T, TOP_K)
    return blk_e, nxt_e, nused, buf_tok, gate.astype(F32), pos2[:, 0], pos2[:, 1]


def _rope_tables(n_tokens, dk):
    freqs = dk // 4
    rows = n_tokens // GRID_W
    row = jnp.repeat(jnp.arange(rows, dtype=F32), GRID_W)
    col = jnp.tile(jnp.arange(GRID_W, dtype=F32), rows)
    inv = 1.0 / (ROPE_THETA ** (jnp.arange(freqs, dtype=F32) / freqs))
    ar, ac = row[:, None] * inv, col[:, None] * inv
    z = jnp.zeros_like(ar)
    cos = jnp.concatenate([jnp.cos(ar), jnp.cos(ar), jnp.cos(ac), jnp.cos(ac)], axis=1)
    s_lo = jnp.concatenate([-jnp.sin(ar), z, -jnp.sin(ac), z], axis=1)
    s_hi = jnp.concatenate([z, jnp.sin(ar), z, jnp.sin(ac)], axis=1)
    return cos, s_lo, s_hi


def kernel(x_prompt, x_sample, cache_k, cache_v, state_gla, c, c_ctx, w_mod, b_mod, w_in, b_gate, da_lambda,
           da_norm_w, gla_a2, gla_a_bias, gla_norm_w, w_da_proj, w_gla_proj, w_out, ln1_g, ln1_b, ln2_g, ln2_b,
           router_g_w, router_g_b, router_e_w, router_e_b, exp_w1, exp_w3, exp_w2):
    B, S, D = x_prompt.shape
    n_dec, dec_seq, _ = x_sample.shape
    depth = w_mod.shape[0]
    past, da_heads, _, dk = cache_k.shape[2:]
    dv = cache_v.shape[-1]
    gla_heads, gk, gv = state_gla.shape[3:]
    lowrank = gla_a2.shape[2]
    n_groups, epg = router_e_w.shape[2:]
    n_exp = n_groups * epg
    da_qw, da_vw = da_heads * 2 * dk, da_heads * dv
    gla_qw, gla_vw = gla_heads * gk, gla_heads * gv
    n_main = 2 * da_qw + da_vw + 2 * gla_qw + 2 * gla_vw
    assert n_main % LANE == 0 and w_in.shape[2] == n_main + 2 * lowrank + 2 * D
    alpha = (2.0 * depth) ** 0.25

    slab = _Slab(B * S, dec_seq, n_dec)
    t_ctx = slab.t_ctx
    moe_tm = 128

    x = jnp.concatenate([x_prompt.reshape(t_ctx, D), x_sample.reshape(n_dec * dec_seq, D)], axis=0)
    n_cond = 8
    cond = jnp.zeros((n_cond, D), F32).at[0].set(c_ctx).at[1:1 + n_dec].set(c)
    w_in_t = jnp.swapaxes(w_in, 1, 2)
    b_gate4 = b_gate.reshape(depth, 2, 1, D)
    nr =-(-(n_groups + n_exp) // LANE) * LANE
    wr = jnp.concatenate([router_g_w, router_e_w.reshape(depth, D, n_exp),
                          jnp.zeros((depth, D, nr - n_groups - n_exp), F32)], axis=2)
    br = jnp.concatenate([router_g_b, router_e_b.reshape(depth, n_exp),
                          jnp.zeros((depth, nr - n_groups - n_exp), F32)], axis=1).reshape(depth, 1, nr)
    a2p = jnp.zeros((depth, 2, 2 * lowrank, gla_qw), F32)
    a2p = a2p.at[:, 0, :lowrank].set(gla_a2[:, 0]).at[:, 1, lowrank:].set(gla_a2[:, 1])
    ab = gla_a_bias.reshape(depth, 2, 1, gla_qw)
    lv = da_lambda.astype(F32)
    lam_init = [0.8 - 0.6 * math.exp(-0.3 * l) for l in range(depth)]
    lam = [(jnp.exp(jnp.sum(lv[l, 0] * lv[l, 1])) - jnp.exp(jnp.sum(lv[l, 2] * lv[l, 3])) + lam_init[l]).reshape(1, 1)
           for l in range(depth)]
    rope = _rope_tables(dec_seq, dk)
    cache_k2 = cache_k.reshape(n_dec, depth, past, da_qw)
    cache_v2 = cache_v.reshape(n_dec, depth, past, da_vw)
    ln1_g3, ln1_b3 = ln1_g.reshape(depth, 1, D), ln1_b.reshape(depth, 1, D)
    ln2_g3, ln2_b3 = ln2_g.reshape(depth, 1, D), ln2_b.reshape(depth, 1, D)

    mod = _modulation(cond, w_mod, b_mod)
    mod5 = mod.reshape(depth, n_cond, 6, 1, D)

    h = _premod(x, mod5, slab, 0)
    ks, vs, sts = [], [], []
    for l in range(depth):
        u, k_ctx, v_ctx = _matmul_t_with_copies(
            h, w_in_t, l, n_main, t_ctx, [(da_qw, 2 * da_qw), (2 * da_qw, 2 * da_qw + da_vw)])
        ag = _matmul_t(h, w_in_t, l, n_main, LANE, tn=LANE)
        gates = _matmul_t(h, w_in_t, l, n_main + 2 * lowrank, 2 * D)
        ks.append(k_ctx.reshape(B, S, da_qw))
        vs.append(v_ctx.reshape(B, S, da_vw))

        post = 1.0 - lam_init[l]
        nw_d = da_norm_w[l].reshape(1, dv)
        od_c = _attn_ctx(u, lam[l], nw_d, n_seq=B, L=S, heads=da_heads, dk=dk, dv=dv, post=post)
        od_d = _attn_dec(u, t_ctx, cache_k2, cache_v2, l, rope, lam[l], nw_d,
                         n_seq=n_dec, L=dec_seq, heads=da_heads, dk=dk, dv=dv, post=post)

        qcol = 2 * da_qw + da_vw
        vcol = qcol + 2 * gla_qw
        rcol = vcol + gla_vw
        nw_g = gla_norm_w[l].reshape(1, gv)
        gla_kw = dict(heads=gla_heads, K=gk, V=gv, qcol=qcol, vcol=vcol, rcol=rcol)
        og_c, st = _gla(u, ag, 0, a2p[l], ab[l], nw_g, None, n_seq=B, L=S, want_state=True, **gla_kw)
        og_d, _ = _gla(u, ag, t_ctx, a2p[l], ab[l], nw_g, state_gla[:, l], n_seq=n_dec, L=dec_seq,
                       want_state=False, **gla_kw)
        sts.append(st)

        m = _merge(od_c, od_d, og_c, og_d, w_da_proj, w_gla_proj, gates, b_gate4, l)
        y = _matmul(m, w_out, l, 0, D)
        x1, h2, logits = _post_mixer(x, y, mod5, slab, l, ln1_g3, ln1_b3, wr, br, alpha=alpha)

        blk_e, nxt_e, nused, buf_tok, gate, p0, p1 = _route(logits, n_groups, epg, moe_tm)
        es = _moe(h2, blk_e, nxt_e, nused, buf_tok, exp_w1, exp_w3, exp_w2, l, tm=moe_tm)
        x, h = _post_moe(p0, p1, es, x1, gate, mod5, slab, l, ln2_g3, ln2_b3, alpha=alpha, has_next=l + 1 < depth)

    y_prompt = x[:t_ctx].reshape(B, S, D)
    y_sample = x[t_ctx:].reshape(n_dec, dec_seq, D)
    new_k = jnp.stack(ks, axis=1).reshape(B, depth, S, da_heads, 2, dk)
    new_v = jnp.stack(vs, axis=1).reshape(B, depth, S, da_heads, dv)
    return (y_prompt, y_sample, new_k, new_v, jnp.stack(sts, axis=1))
```

```python
import functools
import math

import jax
import jax.numpy as jnp
from jax import lax
from jax.experimental import pallas as pl
from jax.experimental.pallas import tpu as pltpu

F32 = jnp.float32
BF16 = jnp.bfloat16

GRID_W = 64
ROPE_THETA = 10000.0
GLA_TAU = 16.0
GLA_CHUNK = 64
TOP_K = 2
LN_EPS = 1e-5
LANE = 128
VMEM_LIMIT = 56 * 1024 * 1024
DMA_UNROLL = 8
ROW_SEMS = 4


def _cparams(n_axes):
    return pltpu.CompilerParams(dimension_semantics=("arbitrary",) * n_axes, vmem_limit_bytes=VMEM_LIMIT)


def _tile(n, pref):
    t = min(n, pref)
    while n % t:
        t //= 2
    return t


def _silu(x):
    return x * (1.0 / (1.0 + jnp.exp(-x)))


def _sigmoid(x):
    return 1.0 / (1.0 + jnp.exp(-x))


def _mm_kernel(x_ref, w_ref, o_ref, wb_ref):
    @pl.when(pl.program_id(1) == 0)
    def _():
        wb_ref[...] = w_ref[...].astype(BF16)

    o_ref[...] = jnp.dot(x_ref[...], wb_ref[...], preferred_element_type=F32).astype(o_ref.dtype)


def _matmul(x, w, l, col0, ncols, *, tm=1024, tn=512, out_dtype=F32):
    M, K = x.shape
    tm = _tile(M, tm)
    tn = _tile(ncols, tn)
    assert col0 % tn == 0
    cb0 = col0 // tn
    return pl.pallas_call(
        _mm_kernel,
        grid=(ncols // tn, M // tm),
        in_specs=[
            pl.BlockSpec((tm, K), lambda j, i: (i, 0)),
            pl.BlockSpec((None, K, tn), lambda j, i: (l, 0, cb0 + j)),
        ],
        out_specs=pl.BlockSpec((tm, tn), lambda j, i: (i, j)),
        out_shape=jax.ShapeDtypeStruct((M, ncols), out_dtype),
        scratch_shapes=[pltpu.VMEM((K, tn), BF16)],
        compiler_params=_cparams(2),
    )(x, w)


_NT = (((1,), (1,)), ((), ()))


def _mm_t_kernel(x_ref, w_ref, o_ref, wb_ref):
    @pl.when(pl.program_id(1) == 0)
    def _():
        wb_ref[...] = w_ref[...].astype(BF16)

    o_ref[...] = lax.dot_general(x_ref[...], wb_ref[...], _NT, preferred_element_type=F32).astype(o_ref.dtype)


def _mm_t_shift_kernel(x_ref, wa_ref, wt_ref, o_ref, wb_ref, *, shift):
    @pl.when(pl.program_id(1) == 0)
    def _():
        tn = wb_ref.shape[0]
        wb_ref[0:tn - shift, :] = wa_ref[shift:, :].astype(BF16)
        wb_ref[tn - shift:, :] = wt_ref[:shift, :].astype(BF16)

    o_ref[...] = lax.dot_general(x_ref[...], wb_ref[...], _NT, preferred_element_type=F32).astype(o_ref.dtype)


def _mm_t_copies_kernel(x_ref, w_ref, o_ref, *rest, n_rows, windows):
    copy_refs, wb_ref = rest[:-1], rest[-1]
    j, i = pl.program_id(0), pl.program_id(1)

    @pl.when(i == 0)
    def _():
        wb_ref[...] = w_ref[...].astype(BF16)

    acc = lax.dot_general(x_ref[...], wb_ref[...], _NT, preferred_element_type=F32)
    o_ref[...] = acc
    for c_ref, (j0, j1) in zip(copy_refs, windows):
        @pl.when(jnp.logical_and(i < n_rows, jnp.logical_and(j >= j0, j < j1)))
        def _(c_ref=c_ref):
            c_ref[...] = acc


def _matmul_t_with_copies(x, wt, l, nrows, copy_rows, copy_cols, *, tm=1024, tn=512):
    M, K = x.shape
    tm = _tile(math.gcd(M, copy_rows), tm)
    tn = _tile(nrows, tn)
    n_rows = copy_rows // tm
    windows = []
    for c0, c1 in copy_cols:
        assert c0 % tn == 0 and c1 % tn == 0
        windows.append((c0 // tn, c1 // tn))

    def copy_spec(j0, j1):
        def imap(j, i):
            inside = jnp.logical_and(j >= j0, j < j1)
            row = jnp.where(j < j0, 0, jnp.where(inside, jnp.minimum(i, n_rows - 1), n_rows - 1))
            col = jnp.where(j < j0, 0, jnp.where(inside, j - j0, j1 - j0 - 1))
            return (row, col)
        return pl.BlockSpec((tm, tn), imap)

    return pl.pallas_call(
        functools.partial(_mm_t_copies_kernel, n_rows=n_rows, windows=tuple(windows)),
        grid=(nrows // tn, M // tm),
        in_specs=[
            pl.BlockSpec((tm, K), lambda j, i: (i, 0)),
            pl.BlockSpec((None, tn, K), lambda j, i: (l, j, 0)),
        ],
        out_specs=[pl.BlockSpec((tm, tn), lambda j, i: (i, j))] + [copy_spec(j0, j1) for j0, j1 in windows],
        out_shape=[jax.ShapeDtypeStruct((M, nrows), F32)]
        + [jax.ShapeDtypeStruct((copy_rows, c1 - c0), F32) for c0, c1 in copy_cols],
        scratch_shapes=[pltpu.VMEM((tn, K), BF16)],
        compiler_params=_cparams(2),
    )(x, wt)


def _matmul_t(x, wt, l, row0, nrows, *, tm=1024, tn=512, out_dtype=F32):
    M, K = x.shape
    tm = _tile(M, tm)
    tn = _tile(nrows, tn)
    base = row0 // LANE * LANE
    shift = row0 - base
    assert base % tn == 0 and tn % LANE == 0 and shift % 32 == 0
    rb0 = base // tn
    in_specs = [
        pl.BlockSpec((tm, K), lambda j, i: (i, 0)),
        pl.BlockSpec((None, tn, K), lambda j, i: (l, rb0 + j, 0)),
    ]
    args = [x, wt]
    kern = _mm_t_kernel
    if shift:
        lpt = tn // LANE
        in_specs.append(pl.BlockSpec((None, LANE, K), lambda j, i: (l, (rb0 + j + 1) * lpt, 0)))
        args.append(wt)
        kern = functools.partial(_mm_t_shift_kernel, shift=shift)
    return pl.pallas_call(
        kern,
        grid=(nrows // tn, M // tm),
        in_specs=in_specs,
        out_specs=pl.BlockSpec((tm, tn), lambda j, i: (i, j)),
        out_shape=jax.ShapeDtypeStruct((M, nrows), out_dtype),
        scratch_shapes=[pltpu.VMEM((tn, K), BF16)],
        compiler_params=_cparams(2),
    )(*args)


def _mod_kernel(c_ref, w_ref, b_ref, o_ref):
    a = _silu(c_ref[...]).astype(BF16)
    o_ref[...] = jnp.dot(a, w_ref[...].astype(BF16), preferred_element_type=F32) + b_ref[...]


def _modulation(cond, w_mod, b_mod, *, tn=512):
    depth, D, N = w_mod.shape
    R = cond.shape[0]
    tn = _tile(N, tn)
    return pl.pallas_call(
        _mod_kernel,
        grid=(depth, N // tn),
        in_specs=[
            pl.BlockSpec((R, D), lambda l, j: (0, 0)),
            pl.BlockSpec((None, D, tn), lambda l, j: (l, 0, j)),
            pl.BlockSpec((None, 1, tn), lambda l, j: (l, 0, j)),
        ],
        out_specs=pl.BlockSpec((None, R, tn), lambda l, j: (l, 0, j)),
        out_shape=jax.ShapeDtypeStruct((depth, R, N), F32),
        compiler_params=_cparams(2),
    )(cond, w_mod, b_mod.reshape(depth, 1, N))


class _Slab:
    def __init__(self, t_ctx, dec_seq, n_dec):
        self.t_ctx, self.dec_seq, self.n_dec = t_ctx, dec_seq, n_dec
        self.T = t_ctx + dec_seq * n_dec

    def tile(self, pref):
        t = min(pref, self.t_ctx, self.dec_seq)
        while self.t_ctx % t or self.dec_seq % t:
            t //= 2
        return t

    def ctx_spec(self, tm, D):
        n_ctx = self.t_ctx // tm
        return pl.BlockSpec((tm, D), lambda i, *_: (jnp.minimum(i, n_ctx - 1), 0))

    def dec_spec(self, tm, D):
        n_ctx = self.t_ctx // tm
        return pl.BlockSpec((tm, D), lambda i, *_: (jnp.maximum(i - n_ctx, 0), 0))

    def mod_spec(self, tm, l, which, D):
        n_ctx = self.t_ctx // tm
        per = self.dec_seq // tm

        def imap(i, *_):
            return (l, jnp.where(i < n_ctx, 0, (i - n_ctx) // per + 1), which, 0, 0)

        return pl.BlockSpec((None, None, None, 1, D), imap)


def _premod_kernel(xc_ref, xd_ref, sc_ref, sh_ref, h_ref, *, n_ctx):
    x = _pick_rows(xc_ref, xd_ref, n_ctx)
    h_ref[...] = (x * (1.0 + sc_ref[...]) + sh_ref[...]).astype(h_ref.dtype)


def _pick_rows(c_ref, d_ref, n_ctx):
    return jnp.where(pl.program_id(0) < n_ctx, c_ref[...], d_ref[...])


def _premod(x_ctx, x_dec, mod5, slab, l):
    D = x_ctx.shape[1]
    tm = slab.tile(256)
    return pl.pallas_call(
        functools.partial(_premod_kernel, n_ctx=slab.t_ctx // tm),
        grid=(slab.T // tm,),
        in_specs=[slab.ctx_spec(tm, D), slab.dec_spec(tm, D), slab.mod_spec(tm, l, 1, D), slab.mod_spec(tm, l, 0, D)],
        out_specs=pl.BlockSpec((tm, D), lambda i: (i, 0)),
        out_shape=jax.ShapeDtypeStruct((slab.T, D), BF16),
        compiler_params=_cparams(1),
    )(x_ctx, x_dec, mod5, mod5)


def _exp_rows(s):
    e = jnp.exp(s - jnp.max(s, axis=-1, keepdims=True))
    return e, jnp.sum(e, axis=-1, keepdims=True)


def _diff_attn_rows(q, k1, k2, v, lam, scale, dk):
    nt = (((1,), (1,)), ((), ()))
    qs = q * scale
    e1, z1 = _exp_rows(lax.dot_general(qs[:, :dk].astype(BF16), k1, nt, preferred_element_type=F32))
    e2, z2 = _exp_rows(lax.dot_general(qs[:, dk:].astype(BF16), k2, nt, preferred_element_type=F32))
    o1 = jnp.dot(e1.astype(BF16), v, preferred_element_type=F32)
    o2 = jnp.dot(e2.astype(BF16), v, preferred_element_type=F32)
    return o1 * (1.0 / z1) - o2 * (lam / z2)


def _rms_rows(o, w):
    return o * lax.rsqrt(jnp.mean(o * o, axis=-1, keepdims=True) + LN_EPS) * w


def _attn_ctx_kernel(lam_ref, q_ref, k_ref, v_ref, nw_ref, o_ref, *, heads, dk, dv, post):
    lam = lam_ref[0, 0]
    scale = dk ** -0.5
    for h in range(heads):
        q = q_ref[:, h * 2 * dk:(h + 1) * 2 * dk]
        k = k_ref[:, h * 2 * dk:(h + 1) * 2 * dk].astype(BF16)
        v = v_ref[:, h * dv:(h + 1) * dv].astype(BF16)
        o = _diff_attn_rows(q, k[:, :dk], k[:, dk:], v, lam, scale, dk)
        o_ref[:, h * dv:(h + 1) * dv] = (_rms_rows(o, nw_ref[...]) * post).astype(o_ref.dtype)


def _attn_ctx(u, lam, norm_w, *, n_seq, L, heads, dk, dv, post):
    qw = heads * 2 * dk
    vw = heads * dv
    assert qw == vw
    kern = functools.partial(_attn_ctx_kernel, heads=heads, dk=dk, dv=dv, post=post)
    return pl.pallas_call(
        kern,
        grid=(n_seq,),
        in_specs=[
            pl.BlockSpec(memory_space=pltpu.SMEM),
            pl.BlockSpec((L, qw), lambda b: (b, 0)),
            pl.BlockSpec((L, qw), lambda b: (b, 1)),
            pl.BlockSpec((L, vw), lambda b: (b, 2)),
            pl.BlockSpec((1, dv), lambda b: (0, 0)),
        ],
        out_specs=pl.BlockSpec((L, vw), lambda b: (b, 0)),
        out_shape=jax.ShapeDtypeStruct((n_seq * L, vw), BF16),
        compiler_params=_cparams(1),
    )(lam, u, u, u, norm_w)


def _rope_rows(x, cos, s_lo, s_hi, dk):
    q4 = dk // 4
    return x * cos + pltpu.roll(x, dk - q4, 1) * s_lo + pltpu.roll(x, q4, 1) * s_hi


def _attn_dec_kernel(lam_ref, q_ref, k_ref, v_ref, ck_ref, cv_ref, cos_ref, slo_ref, shi_ref, nw_ref, o_ref,
                     kb_ref, vb_ref, *, dk, dv, past, L, tq, post):
    lam = lam_ref[0, 0]
    scale = dk ** -0.5
    cos, slo, shi = cos_ref[...], slo_ref[...], shi_ref[...]
    kb_ref[0:past, :] = ck_ref[...].astype(BF16)
    vb_ref[0:past, :] = cv_ref[...].astype(BF16)
    for c in range(2):
        kb_ref[past:past + L, c * dk:(c + 1) * dk] = _rope_rows(
            k_ref[:, c * dk:(c + 1) * dk], cos, slo, shi, dk).astype(BF16)
    vb_ref[past:past + L, :] = v_ref[...].astype(BF16)
    k1 = kb_ref[:, :dk]
    k2 = kb_ref[:, dk:]
    v = vb_ref[...]
    for i in range(L // tq):
        rows = slice(i * tq, (i + 1) * tq)
        q = jnp.concatenate(
            [_rope_rows(q_ref[rows, c * dk:(c + 1) * dk], cos[rows], slo[rows], shi[rows], dk) for c in range(2)],
            axis=1)
        o = _diff_attn_rows(q, k1, k2, v, lam, scale, dk)
        o_ref[rows, :] = (_rms_rows(o, nw_ref[...]) * post).astype(o_ref.dtype)


def _attn_dec(u, row0, cache_k, cache_v, l, rope, lam, norm_w, *, n_seq, L, heads, dk, dv, post):
    past = cache_k.shape[2]
    assert row0 % L == 0
    rb0 = row0 // L
    tq = _tile(L, 256)
    kern = functools.partial(_attn_dec_kernel, dk=dk, dv=dv, past=past, L=L, tq=tq, post=post)
    cos, slo, shi = rope
    tab = pl.BlockSpec((L, dk), lambda b, h: (0, 0))
    return pl.pallas_call(
        kern,
        grid=(n_seq, heads),
        in_specs=[
            pl.BlockSpec(memory_space=pltpu.SMEM),
            pl.BlockSpec((L, 2 * dk), lambda b, h: (rb0 + b, h)),
            pl.BlockSpec((L, 2 * dk), lambda b, h: (rb0 + b, heads + h)),
            pl.BlockSpec((L, dv), lambda b, h: (rb0 + b, 2 * heads + h)),
            pl.BlockSpec((None, None, past, 2 * dk), lambda b, h: (b, l, 0, h)),
            pl.BlockSpec((None, None, past, dv), lambda b, h: (b, l, 0, h)),
            tab, tab, tab,
            pl.BlockSpec((1, dv), lambda b, h: (0, 0)),
        ],
        out_specs=pl.BlockSpec((L, dv), lambda b, h: (b, h)),
        out_shape=jax.ShapeDtypeStruct((n_seq * L, heads * dv), BF16),
        scratch_shapes=[pltpu.VMEM((past + L, 2 * dk), BF16), pltpu.VMEM((past + L, dv), BF16)],
        compiler_params=_cparams(2),
    )(lam, u, u, u, cache_k, cache_v, cos, slo, shi, norm_w)


def _log_sigmoid(x):
    return -(jnp.maximum(-x, 0.0) + jnp.log1p(jnp.exp(-jnp.abs(x))))


def _split3_bf16(x):
    hi = x.astype(BF16)
    r1 = x - hi.astype(F32)
    mid = r1.astype(BF16)
    lo = (r1 - mid.astype(F32)).astype(BF16)
    return hi, mid, lo


def _gla_kernel(*refs, L, K, V, G, has_s0, want_state):
    q_ref, k_ref, v_ref, r_ref, ag_ref, a2_ref, ab_ref, nw_ref = refs[:8]
    refs = refs[8:]
    s0_ref = None
    if has_s0:
        s0_ref, refs = refs[0], refs[1:]
    o_ref, refs = refs[0], refs[1:]
    sfin_ref = None
    if want_state:
        sfin_ref, refs = refs[0], refs[1:]
    st_ref, of_ref, ob_ref = refs

    C = GLA_CHUNK
    N = L // C
    r2 = a2_ref.shape[1]
    row = lax.broadcasted_iota(jnp.int32, (C, C), 0)
    col = lax.broadcasted_iota(jnp.int32, (C, C), 1)
    nt = (((1,), (1,)), ((), ()))
    tn = (((0,), (0,)), ((), ()))
    keeps = (col <= row, col >= row)
    tris = tuple(jnp.where(kp, 1.0, 0.0).astype(BF16) for kp in keeps)

    for d in range(2):
        for g in range(G):
            st_ref[d, g] = s0_ref[d, g].T if has_s0 else jnp.zeros((V, K), F32)

    o_refs = (of_ref, ob_ref)

    def body(i, carry):
        sls, q_in, k_in, k_out, dec = [], [], [], [], []
        for d in range(2):
            n = i if d == 0 else N - 1 - i
            sl = pl.ds(pl.multiple_of(n * C, C), C)
            agc = ag_ref[sl, :][:, :r2].astype(BF16)
            logit = jnp.dot(agc, a2_ref[d].astype(BF16), preferred_element_type=F32) + ab_ref[d]
            la = _log_sigmoid(logit) * (1.0 / GLA_TAU)
            hi, mid, lo = _split3_bf16(la)
            bc = (jnp.dot(tris[d], hi, preferred_element_type=F32) + jnp.dot(tris[d], mid, preferred_element_type=F32)
                  + jnp.dot(tris[d], lo, preferred_element_type=F32))
            blast = bc[C - 1:C, :] if d == 0 else bc[0:1, :]
            kc = k_ref[sl, :]
            sls.append(sl)
            q_in.append((q_ref[sl, :] * (K ** -0.5) * jnp.exp(bc)).astype(BF16))
            k_in.append((kc * jnp.exp(-bc)).astype(BF16))
            k_out.append((kc * jnp.exp(blast - bc)).astype(BF16))
            dec.append(jnp.exp(blast))
        pairs = [(d, g) for d in range(2) for g in range(G)]
        ksl = lambda g: slice(g * K, (g + 1) * K)
        vsl = lambda g: slice(g * V, (g + 1) * V)
        att = {(d, g): lax.dot_general(q_in[d][:, ksl(g)], k_in[d][:, ksl(g)], nt, preferred_element_type=F32)
               for d, g in pairs}
        for d, g in pairs:
            o_refs[d][sls[d], vsl(g)] = lax.dot_general(q_in[d][:, ksl(g)], st_ref[d, g].astype(BF16), nt,
                                                        preferred_element_type=F32)
        for d, g in pairs:
            vc = v_ref[sls[d], vsl(g)].astype(BF16)
            ds_t = lax.dot_general(vc, k_out[d][:, ksl(g)], tn, preferred_element_type=F32)
            st_ref[d, g] = dec[d][:, ksl(g)] * st_ref[d, g] + ds_t
        for d, g in pairs:
            vc = v_ref[sls[d], vsl(g)].astype(BF16)
            a = jnp.where(keeps[d], att[d, g], 0.0).astype(BF16)
            o_refs[d][sls[d], vsl(g)] += jnp.dot(a, vc, preferred_element_type=F32)
        return carry

    lax.fori_loop(0, N, body, 0)
    if want_state:
        for d in range(2):
            for g in range(G):
                sfin_ref[d, g] = st_ref[d, g].T
    for g in range(G):
        vs = slice(g * V, (g + 1) * V)
        og = _rms_rows(of_ref[:, vs] + ob_ref[:, vs], nw_ref[...]) * _silu(r_ref[:, vs])
        o_ref[:, vs] = og.astype(o_ref.dtype)


def _gla(u, ag, row0, a2p, ab, norm_w, s0, *, n_seq, L, heads, K, V, qcol, vcol, rcol, want_state):
    G = _tile(heads, 4)
    gk, gv = G * K, G * V
    assert row0 % L == 0 and qcol % gk == 0 and (heads * K) % gk == 0 and vcol % gv == 0 and rcol % gv == 0
    rb0 = row0 // L
    qb, kb, vb, rb = qcol // gk, (qcol + heads * K) // gk, vcol // gv, rcol // gv
    r2 = a2p.shape[1]
    has_s0 = s0 is not None
    kern = functools.partial(_gla_kernel, L=L, K=K, V=V, G=G, has_s0=has_s0, want_state=want_state)
    in_specs = [
        pl.BlockSpec((L, gk), lambda b, h: (rb0 + b, qb + h)),
        pl.BlockSpec((L, gk), lambda b, h: (rb0 + b, kb + h)),
        pl.BlockSpec((L, gv), lambda b, h: (rb0 + b, vb + h)),
        pl.BlockSpec((L, gv), lambda b, h: (rb0 + b, rb + h)),
        pl.BlockSpec((L, LANE), lambda b, h: (rb0 + b, 0)),
        pl.BlockSpec((2, r2, gk), lambda b, h: (0, 0, h)),
        pl.BlockSpec((2, 1, gk), lambda b, h: (0, 0, h)),
        pl.BlockSpec((1, V), lambda b, h: (0, 0)),
    ]
    args = [u, u, u, u, ag, a2p, ab, norm_w]
    if has_s0:
        in_specs.append(pl.BlockSpec((None, 2, G, K, V), lambda b, h: (b, 0, h, 0, 0)))
        args.append(s0)
    out_specs = [pl.BlockSpec((L, gv), lambda b, h: (b, h))]
    out_shape = [jax.ShapeDtypeStruct((n_seq * L, heads * V), BF16)]
    if want_state:
        out_specs.append(pl.BlockSpec((None, 2, G, K, V), lambda b, h: (b, 0, h, 0, 0)))
        out_shape.append(jax.ShapeDtypeStruct((n_seq, 2, heads, K, V), F32))
    res = pl.pallas_call(
        kern,
        grid=(n_seq, heads // G),
        in_specs=in_specs,
        out_specs=out_specs,
        out_shape=out_shape,
        scratch_shapes=[pltpu.VMEM((2, G, V, K), F32), pltpu.VMEM((L, gv), F32), pltpu.VMEM((L, gv), F32)],
        compiler_params=_cparams(2),
    )(*args)
    return res if want_state else (res[0], None)


def _merge_kernel(odc_ref, odd_ref, ogc_ref, ogd_ref, wd_ref, wg_ref, gd_ref, gg_ref, bd_ref, bg_ref, o_ref,
                  wdb_ref, wgb_ref, *, n_ctx):
    i = pl.program_id(1)

    @pl.when(i == 0)
    def _():
        wdb_ref[...] = wd_ref[...].astype(BF16)
        wgb_ref[...] = wg_ref[...].astype(BF16)

    def run(od_ref, og_ref):
        yd = jnp.dot(od_ref[...], wdb_ref[...], preferred_element_type=F32)
        yg = jnp.dot(og_ref[...], wgb_ref[...], preferred_element_type=F32)
        m = _sigmoid(gd_ref[...] + bd_ref[...]) * yd + _sigmoid(gg_ref[...] + bg_ref[...]) * yg
        o_ref[...] = m.astype(o_ref.dtype)

    @pl.when(i < n_ctx)
    def _():
        run(odc_ref, ogc_ref)

    @pl.when(i >= n_ctx)
    def _():
        run(odd_ref, ogd_ref)


def _merge(od_c, od_d, og_c, og_d, w_da, w_gla, gates, b_gate4, l, *, tm=512, tn=512):
    t_ctx, Kd = od_c.shape
    t_dec = od_d.shape[0]
    Kg = og_c.shape[1]
    D = w_da.shape[2]
    tm = _tile(math.gcd(t_ctx, t_dec), tm)
    tn = _tile(D, tn)
    nj = D // tn
    n_ctx = t_ctx // tm
    T = t_ctx + t_dec
    ctx_rows = lambda j, i: (jnp.minimum(i, n_ctx - 1), 0)
    dec_rows = lambda j, i: (jnp.maximum(i - n_ctx, 0), 0)
    return pl.pallas_call(
        functools.partial(_merge_kernel, n_ctx=n_ctx),
        grid=(nj, T // tm),
        in_specs=[
            pl.BlockSpec((tm, Kd), ctx_rows),
            pl.BlockSpec((tm, Kd), dec_rows),
            pl.BlockSpec((tm, Kg), ctx_rows),
            pl.BlockSpec((tm, Kg), dec_rows),
            pl.BlockSpec((None, Kd, tn), lambda j, i: (l, 0, j)),
            pl.BlockSpec((None, Kg, tn), lambda j, i: (l, 0, j)),
            pl.BlockSpec((tm, tn), lambda j, i: (i, j)),
            pl.BlockSpec((tm, tn), lambda j, i: (i, nj + j)),
            pl.BlockSpec((None, None, 1, tn), lambda j, i: (l, 0, 0, j)),
            pl.BlockSpec((None, None, 1, tn), lambda j, i: (l, 1, 0, j)),
        ],
        out_specs=pl.BlockSpec((tm, tn), lambda j, i: (i, j)),
        out_shape=jax.ShapeDtypeStruct((T, D), BF16),
        scratch_shapes=[pltpu.VMEM((Kd, tn), BF16), pltpu.VMEM((Kg, tn), BF16)],
        compiler_params=_cparams(2),
    )(od_c, od_d, og_c, og_d, w_da, w_gla, gates, gates, b_gate4, b_gate4)


def _layer_norm_rows(z, g, b):
    mu = jnp.mean(z, axis=-1, keepdims=True)
    zc = z - mu
    var = jnp.mean(zc * zc, axis=-1, keepdims=True)
    return zc * lax.rsqrt(var + LN_EPS) * g + b


def _post_mixer_kernel(xc_ref, xd_ref, y_ref, g1_ref, sc_ref, sh_ref, lg_ref, lb_ref, wr_ref, br_ref,
                       x1_ref, h2_ref, lo_ref, *, alpha, n_ctx):
    x = _pick_rows(xc_ref, xd_ref, n_ctx)
    x1 = _layer_norm_rows(alpha * x + g1_ref[...] * y_ref[...], lg_ref[...], lb_ref[...])
    x1_ref[...] = x1
    h2 = x1 * (1.0 + sc_ref[...]) + sh_ref[...]
    h2_ref[...] = h2.reshape(h2_ref.shape)
    lo_ref[...] = jnp.dot(h2, wr_ref[...], preferred_element_type=F32,
                          precision=lax.Precision.HIGHEST) + br_ref[...]


def _post_mixer(x_ctx, x_dec, y, mod5, slab, l, ln_g, ln_b, wr, br, *, alpha):
    T, D = y.shape
    tm = slab.tile(256)
    nr = wr.shape[2]
    row = pl.BlockSpec((tm, D), lambda i: (i, 0))
    vec = pl.BlockSpec((None, 1, D), lambda i: (l, 0, 0))
    return pl.pallas_call(
        functools.partial(_post_mixer_kernel, alpha=alpha, n_ctx=slab.t_ctx // tm),
        grid=(T // tm,),
        in_specs=[slab.ctx_spec(tm, D), slab.dec_spec(tm, D), row,
                  slab.mod_spec(tm, l, 2, D), slab.mod_spec(tm, l, 4, D), slab.mod_spec(tm, l, 3, D),
                  vec, vec,
                  pl.BlockSpec((None, D, nr), lambda i: (l, 0, 0)),
                  pl.BlockSpec((None, 1, nr), lambda i: (l, 0, 0))],
        out_specs=[row, pl.BlockSpec((tm, 1, D), lambda i: (i, 0, 0)), pl.BlockSpec((tm, nr), lambda i: (i, 0))],
        out_shape=[jax.ShapeDtypeStruct((T, D), F32), jax.ShapeDtypeStruct((T, 1, D), F32),
                   jax.ShapeDtypeStruct((T, nr), F32)],
        compiler_params=_cparams(1),
    )(x_ctx, x_dec, y, mod5, mod5, mod5, ln_g, ln_b, wr, br)


def _issue_rows(copy_fn, tm, priority=None):
    def f(i, c):
        for q in range(ROW_SEMS):
            for cp in copy_fn(i * ROW_SEMS + q, q):
                cp.start(priority=q % 2 if priority is None else priority)
        return c
    lax.fori_loop(0, tm // ROW_SEMS, f, 0, unroll=DMA_UNROLL // ROW_SEMS)


def _wait_rows(src_hbm, dst, sems):
    n = dst.shape[0] // ROW_SEMS
    for q in range(ROW_SEMS):
        pltpu.make_async_copy(src_hbm.at[pl.ds(0, n)], dst.at[pl.ds(0, n)], sems.at[q]).wait()


def _expert_changed(blk_e_ref, b):
    return jnp.logical_or(b == 0, blk_e_ref[b] != blk_e_ref[jnp.maximum(b - 1, 0)])


def _moe_ffn_kernel(blk_e_ref, nxt_e_ref, nused_ref, tok_ref, x_hbm, w1_hbm, w3_hbm, w2_hbm, o_ref,
                    rows_ref, dense_ref, s1_ref, s3_ref, s2_ref, b1_ref, b3_ref, b2_ref, sem, row_sem, *, l, tm):
    b = pl.program_id(0)
    nused = nused_ref[0]
    used = b < nused
    streams = ((w1_hbm, s1_ref, b1_ref), (w3_hbm, s3_ref, b3_ref), (w2_hbm, s2_ref, b2_ref))

    def issue_rows(blk, slot):
        def copy_fn(r, q):
            t = tok_ref[blk * tm + r]
            return (pltpu.make_async_copy(x_hbm.at[pl.ds(t, 1)], rows_ref.at[slot, pl.ds(r, 1)],
                                          row_sem.at[slot, q]),)
        _issue_rows(copy_fn, tm, priority=1)

    @pl.when(jnp.logical_and(b == 0, nused > 0))
    def _():
        issue_rows(0, 0)

    @pl.when(b + 1 < nused)
    def _():
        issue_rows(b + 1, (b + 1) % 2)

    def fetch(k, e):
        w_hbm, stage_ref, _ = streams[k]
        return pltpu.make_async_copy(w_hbm.at[l, e], stage_ref, sem.at[k])

    @pl.when(jnp.logical_and(used, _expert_changed(blk_e_ref, b)))
    def _():
        e = blk_e_ref[b]
        nxt = nxt_e_ref[b]

        @pl.when(b == 0)
        def _():
            for k in range(3):
                fetch(k, e).start()

        for k in range(3):
            _, stage_ref, cast_ref = streams[k]
            fetch(k, e).wait()
            cast_ref[...] = stage_ref[...].astype(BF16)

            @pl.when(nxt >= 0)
            def _(k=k):
                fetch(k, nxt).start()

    @pl.when(used)
    def _():
        _wait_rows(x_hbm, rows_ref.at[b % 2], row_sem.at[b % 2])
        dense_ref[...] = rows_ref[b % 2].reshape(dense_ref.shape)
        xs = dense_ref[...].astype(BF16)
        h1 = jnp.dot(xs, b1_ref[...], preferred_element_type=F32)
        h3 = jnp.dot(xs, b3_ref[...], preferred_element_type=F32)
        hid = (_silu(h1) * h3).astype(BF16)
        o_ref[...] = jnp.dot(hid, b2_ref[...], preferred_element_type=F32)

    @pl.when(jnp.logical_not(used))
    def _():
        o_ref[...] = jnp.zeros_like(o_ref)


def _moe(h2, blk_e, nxt_e, nused, buf_tok, w1, w3, w2, l, *, tm):
    D, FF = w1.shape[2:]
    P = buf_tok.shape[0]
    grid_spec = pltpu.PrefetchScalarGridSpec(
        num_scalar_prefetch=4,
        grid=(P // tm,),
        in_specs=[pl.BlockSpec(memory_space=pl.ANY)] * 4,
        out_specs=pl.BlockSpec((tm, D), lambda b, *_: (b, 0)),
        scratch_shapes=[pltpu.VMEM((2, tm, 1, D), F32), pltpu.VMEM((tm, D), F32),
                        pltpu.VMEM((D, FF), F32), pltpu.VMEM((D, FF), F32), pltpu.VMEM((FF, D), F32),
                        pltpu.VMEM((D, FF), BF16), pltpu.VMEM((D, FF), BF16), pltpu.VMEM((FF, D), BF16),
                        pltpu.SemaphoreType.DMA((3,)), pltpu.SemaphoreType.DMA((2, ROW_SEMS))],
    )
    return pl.pallas_call(
        functools.partial(_moe_ffn_kernel, l=l, tm=tm),
        grid_spec=grid_spec,
        out_shape=jax.ShapeDtypeStruct((P, D), F32),
        compiler_params=_cparams(1),
    )(blk_e, nxt_e, nused, buf_tok, h2, w1, w3, w2)


def _post_moe_kernel(p0_ref, p1_ref, es_hbm, x_ref, gw_ref, g2_ref, lg_ref, lb_ref, *rest, tm, alpha, has_next,
                     n_ctx):
    if has_next:
        sc_ref, sh_ref, x2c_ref, x2d_ref, h_ref, buf_ref, sem = rest
    else:
        x2c_ref, x2d_ref, buf_ref, sem = rest
    i = pl.program_id(0)
    n = pl.num_programs(0)
    p_refs = (p0_ref, p1_ref)

    def issue(tile, slot):
        def copy_fn(r, q):
            return tuple(pltpu.make_async_copy(es_hbm.at[pl.ds(p_refs[j][tile * tm + r], 1)],
                                               buf_ref.at[slot, j, pl.ds(r, 1)], sem.at[slot, j, q])
                         for j in range(TOP_K))
        _issue_rows(copy_fn, tm)

    @pl.when(i == 0)
    def _():
        issue(0, 0)

    @pl.when(i + 1 < n)
    def _():
        issue(i + 1, (i + 1) % 2)

    slot = i % 2
    for j in range(TOP_K):
        _wait_rows(es_hbm, buf_ref.at[slot, j], sem.at[slot, j])
    y = gw_ref[:, 0:1] * buf_ref[slot, 0] + gw_ref[:, 1:2] * buf_ref[slot, 1]
    x2 = _layer_norm_rows(alpha * x_ref[...] + g2_ref[...] * y, lg_ref[...], lb_ref[...])

    @pl.when(i < n_ctx)
    def _():
        x2c_ref[...] = x2

    @pl.when(i >= n_ctx)
    def _():
        x2d_ref[...] = x2

    if has_next:
        h_ref[...] = (x2 * (1.0 + sc_ref[...]) + sh_ref[...]).astype(h_ref.dtype)


def _post_moe(p0, p1, es, x1, gate, mod5, slab, l, ln_g, ln_b, *, alpha, has_next):
    T, D = x1.shape
    tm = slab.tile(256)
    row = pl.BlockSpec((tm, D), lambda i, *_: (i, 0))
    vec = pl.BlockSpec((None, 1, D), lambda i, *_: (l, 0, 0))
    in_specs = [pl.BlockSpec(memory_space=pl.ANY), row, pl.BlockSpec((tm, TOP_K), lambda i, *_: (i, 0)),
                slab.mod_spec(tm, l, 5, D), vec, vec]
    args = [es, x1, gate, mod5, ln_g, ln_b]
    out_specs = [slab.ctx_spec(tm, D), slab.dec_spec(tm, D)]
    out_shape = [jax.ShapeDtypeStruct((slab.t_ctx, D), F32), jax.ShapeDtypeStruct((T - slab.t_ctx, D), F32)]
    if has_next:
        in_specs += [slab.mod_spec(tm, l + 1, 1, D), slab.mod_spec(tm, l + 1, 0, D)]
        args += [mod5, mod5]
        out_specs.append(row)
        out_shape.append(jax.ShapeDtypeStruct((T, D), BF16))
    grid_spec = pltpu.PrefetchScalarGridSpec(
        num_scalar_prefetch=2,
        grid=(T // tm,),
        in_specs=in_specs,
        out_specs=out_specs,
        scratch_shapes=[pltpu.VMEM((2, TOP_K, tm, D), F32), pltpu.SemaphoreType.DMA((2, TOP_K, ROW_SEMS))],
    )
    res = pl.pallas_call(
        functools.partial(_post_moe_kernel, tm=tm, alpha=alpha, has_next=has_next, n_ctx=slab.t_ctx // tm),
        grid_spec=grid_spec,
        out_shape=out_shape,
        compiler_params=_cparams(1),
    )(p0, p1, *args)
    return (res[0], res[1], res[2] if has_next else None)


def _route(logits, n_groups, epg, tm):
    T = logits.shape[0]
    n_exp = n_groups * epg
    pg = jax.nn.softmax(logits[:, :n_groups], axis=-1)
    g_idx = jnp.argmax(pg, axis=-1)
    p_sel = jnp.take_along_axis(pg, g_idx[:, None], axis=-1)
    le = logits[:, n_groups:n_groups + n_exp].reshape(T, n_groups, epg)
    le = jnp.take_along_axis(le, g_idx[:, None, None], axis=1)[:, 0]
    pe = jax.nn.softmax(le, axis=-1)
    top_p, top_i = lax.top_k(pe, TOP_K)
    gate = p_sel * top_p / jnp.sum(top_p, axis=-1, keepdims=True)
    expert = (g_idx[:, None] * epg + top_i).astype(jnp.int32)

    A = T * TOP_K
    e_flat = expert.reshape(A)
    tok_flat = jnp.repeat(jnp.arange(T, dtype=jnp.int32), TOP_K)
    onehot = (e_flat[:, None] == jnp.arange(n_exp, dtype=jnp.int32)[None, :]).astype(jnp.int32)
    csum = jnp.cumsum(onehot, axis=0)
    counts = csum[-1]
    rank = jnp.take_along_axis(csum, e_flat[:, None], axis=1)[:, 0] - 1
    padded = (counts + tm - 1) // tm * tm
    pends = jnp.cumsum(padded)
    pstarts = pends - padded
    pos = (pstarts[e_flat] + rank).astype(jnp.int32)
    n_blocks = -(-A // tm) + n_exp
    P = n_blocks * tm
    buf_tok = jnp.zeros((P,), jnp.int32).at[pos].set(tok_flat)
    blk_e = jnp.clip(jnp.searchsorted(pends, jnp.arange(n_blocks, dtype=pends.dtype) * tm, side='right'),
                     0, n_exp - 1).astype(jnp.int32)
    nused = (pends[-1] // tm).astype(jnp.int32).reshape(1)
    ids = jnp.arange(n_exp, dtype=jnp.int32)
    own = jnp.where(counts > 0, ids, n_exp)
    later = lax.cummin(jnp.concatenate([own[1:], jnp.full((1,), n_exp, jnp.int32)]), reverse=True)
    nxt_e = jnp.where(later < n_exp, later, -1)[blk_e].astype(jnp.int32)
    pos2 = pos.reshape(T, TOP_K)
    return blk_e, nxt_e, nused, buf_tok, gate.astype(F32), pos2[:, 0], pos2[:, 1]


def _rope_tables(n_tokens, dk):
    freqs = dk // 4
    rows = n_tokens // GRID_W
    row = jnp.repeat(jnp.arange(rows, dtype=F32), GRID_W)
    col = jnp.tile(jnp.arange(GRID_W, dtype=F32), rows)
    inv = 1.0 / (ROPE_THETA ** (jnp.arange(freqs, dtype=F32) / freqs))
    ar, ac = row[:, None] * inv, col[:, None] * inv
    z = jnp.zeros_like(ar)
    cos = jnp.concatenate([jnp.cos(ar), jnp.cos(ar), jnp.cos(ac), jnp.cos(ac)], axis=1)
    s_lo = jnp.concatenate([-jnp.sin(ar), z, -jnp.sin(ac), z], axis=1)
    s_hi = jnp.concatenate([z, jnp.sin(ar), z, jnp.sin(ac)], axis=1)
    return cos, s_lo, s_hi


def kernel(x_prompt, x_sample, cache_k, cache_v, state_gla, c, c_ctx, w_mod, b_mod, w_in, b_gate, da_lambda,
           da_norm_w, gla_a2, gla_a_bias, gla_norm_w, w_da_proj, w_gla_proj, w_out, ln1_g, ln1_b, ln2_g, ln2_b,
           router_g_w, router_g_b, router_e_w, router_e_b, exp_w1, exp_w3, exp_w2):
    B, S, D = x_prompt.shape
    n_dec, dec_seq, _ = x_sample.shape
    depth = w_mod.shape[0]
    past, da_heads, _, dk = cache_k.shape[2:]
    dv = cache_v.shape[-1]
    gla_heads, gk, gv = state_gla.shape[3:]
    lowrank = gla_a2.shape[2]
    n_groups, epg = router_e_w.shape[2:]
    n_exp = n_groups * epg
    da_qw, da_vw = da_heads * 2 * dk, da_heads * dv
    gla_qw, gla_vw = gla_heads * gk, gla_heads * gv
    n_main = 2 * da_qw + da_vw + 2 * gla_qw + 2 * gla_vw
    assert n_main % LANE == 0 and w_in.shape[2] == n_main + 2 * lowrank + 2 * D
    alpha = (2.0 * depth) ** 0.25

    slab = _Slab(B * S, dec_seq, n_dec)
    t_ctx = slab.t_ctx
    moe_tm = 128

    x_ctx, x_dec = x_prompt.reshape(t_ctx, D), x_sample.reshape(n_dec * dec_seq, D)
    n_cond = 8
    cond = jnp.zeros((n_cond, D), F32).at[0].set(c_ctx).at[1:1 + n_dec].set(c)
    w_in_t = jnp.swapaxes(w_in, 1, 2)
    b_gate4 = b_gate.reshape(depth, 2, 1, D)
    nr =-(-(n_groups + n_exp) // LANE) * LANE
    wr = jnp.concatenate([router_g_w, router_e_w.reshape(depth, D, n_exp),
                          jnp.zeros((depth, D, nr - n_groups - n_exp), F32)], axis=2)
    br = jnp.concatenate([router_g_b, router_e_b.reshape(depth, n_exp),
                          jnp.zeros((depth, nr - n_groups - n_exp), F32)], axis=1).reshape(depth, 1, nr)
    a2p = jnp.zeros((depth, 2, 2 * lowrank, gla_qw), F32)
    a2p = a2p.at[:, 0, :lowrank].set(gla_a2[:, 0]).at[:, 1, lowrank:].set(gla_a2[:, 1])
    ab = gla_a_bias.reshape(depth, 2, 1, gla_qw)
    lv = da_lambda.astype(F32)
    lam_init = [0.8 - 0.6 * math.exp(-0.3 * l) for l in range(depth)]
    lam = [(jnp.exp(jnp.sum(lv[l, 0] * lv[l, 1])) - jnp.exp(jnp.sum(lv[l, 2] * lv[l, 3])) + lam_init[l]).reshape(1, 1)
           for l in range(depth)]
    rope = _rope_tables(dec_seq, dk)
    cache_k2 = cache_k.reshape(n_dec, depth, past, da_qw)
    cache_v2 = cache_v.reshape(n_dec, depth, past, da_vw)
    ln1_g3, ln1_b3 = ln1_g.reshape(depth, 1, D), ln1_b.reshape(depth, 1, D)
    ln2_g3, ln2_b3 = ln2_g.reshape(depth, 1, D), ln2_b.reshape(depth, 1, D)

    mod = _modulation(cond, w_mod, b_mod)
    mod5 = mod.reshape(depth, n_cond, 6, 1, D)

    h = _premod(x_ctx, x_dec, mod5, slab, 0)
    ks, vs, sts = [], [], []
    for l in range(depth):
        u, k_ctx, v_ctx = _matmul_t_with_copies(
            h, w_in_t, l, n_main, t_ctx, [(da_qw, 2 * da_qw), (2 * da_qw, 2 * da_qw + da_vw)])
        ag = _matmul_t(h, w_in_t, l, n_main, LANE, tn=LANE)
        gates = _matmul_t(h, w_in_t, l, n_main + 2 * lowrank, 2 * D)
        ks.append(k_ctx.reshape(B, S, da_qw))
        vs.append(v_ctx.reshape(B, S, da_vw))

        post = 1.0 - lam_init[l]
        nw_d = da_norm_w[l].reshape(1, dv)
        od_c = _attn_ctx(u, lam[l], nw_d, n_seq=B, L=S, heads=da_heads, dk=dk, dv=dv, post=post)
        od_d = _attn_dec(u, t_ctx, cache_k2, cache_v2, l, rope, lam[l], nw_d,
                         n_seq=n_dec, L=dec_seq, heads=da_heads, dk=dk, dv=dv, post=post)

        qcol = 2 * da_qw + da_vw
        vcol = qcol + 2 * gla_qw
        rcol = vcol + gla_vw
        nw_g = gla_norm_w[l].reshape(1, gv)
        gla_kw = dict(heads=gla_heads, K=gk, V=gv, qcol=qcol, vcol=vcol, rcol=rcol)
        og_c, st = _gla(u, ag, 0, a2p[l], ab[l], nw_g, None, n_seq=B, L=S, want_state=True, **gla_kw)
        og_d, _ = _gla(u, ag, t_ctx, a2p[l], ab[l], nw_g, state_gla[:, l], n_seq=n_dec, L=dec_seq,
                       want_state=False, **gla_kw)
        sts.append(st)

        m = _merge(od_c, od_d, og_c, og_d, w_da_proj, w_gla_proj, gates, b_gate4, l)
        y = _matmul(m, w_out, l, 0, D)
        x1, h2, logits = _post_mixer(x_ctx, x_dec, y, mod5, slab, l, ln1_g3, ln1_b3, wr, br, alpha=alpha)

        blk_e, nxt_e, nused, buf_tok, gate, p0, p1 = _route(logits, n_groups, epg, moe_tm)
        es = _moe(h2, blk_e, nxt_e, nused, buf_tok, exp_w1, exp_w3, exp_w2, l, tm=moe_tm)
        x_ctx, x_dec, h = _post_moe(p0, p1, es, x1, gate, mod5, slab, l, ln2_g3, ln2_b3, alpha=alpha,
                                    has_next=l + 1 < depth)

    y_prompt = x_ctx.reshape(B, S, D)
    y_sample = x_dec.reshape(n_dec, dec_seq, D)
    new_k = jnp.stack(ks, axis=1).reshape(B, depth, S, da_heads, 2, dk)
    new_v = jnp.stack(vs, axis=1).reshape(B, depth, S, da_heads, dv)
    return (y_prompt, y_sample, new_k, new_v, jnp.stack(sts, axis=1))
```

```python
import functools
import math

import jax
import jax.numpy as jnp
from jax import lax
from jax.experimental import pallas as pl
from jax.experimental.pallas import tpu as pltpu

F32 = jnp.float32
BF16 = jnp.bfloat16

GRID_W = 64
ROPE_THETA = 10000.0
GLA_TAU = 16.0
GLA_CHUNK = 64
TOP_K = 2
LN_EPS = 1e-5
LANE = 128
VMEM_LIMIT = 56 * 1024 * 1024
DMA_UNROLL = 8
ROW_SEMS = 4


def _cparams(n_axes):
    return pltpu.CompilerParams(dimension_semantics=("arbitrary",) * n_axes, vmem_limit_bytes=VMEM_LIMIT)


def _tile(n, pref):
    t = min(n, pref)
    while n % t:
        t //= 2
    return t


def _silu(x):
    return x * (1.0 / (1.0 + jnp.exp(-x)))


def _sigmoid(x):
    return 1.0 / (1.0 + jnp.exp(-x))


def _mm_kernel(x_ref, w_ref, o_ref, wb_ref):
    @pl.when(pl.program_id(1) == 0)
    def _():
        wb_ref[...] = w_ref[...].astype(BF16)

    o_ref[...] = jnp.dot(x_ref[...], wb_ref[...], preferred_element_type=F32).astype(o_ref.dtype)


def _matmul(x, w, l, col0, ncols, *, tm=1024, tn=512, out_dtype=F32):
    M, K = x.shape
    tm = _tile(M, tm)
    tn = _tile(ncols, tn)
    assert col0 % tn == 0
    cb0 = col0 // tn
    return pl.pallas_call(
        _mm_kernel,
        grid=(ncols // tn, M // tm),
        in_specs=[
            pl.BlockSpec((tm, K), lambda j, i: (i, 0)),
            pl.BlockSpec((None, K, tn), lambda j, i: (l, 0, cb0 + j)),
        ],
        out_specs=pl.BlockSpec((tm, tn), lambda j, i: (i, j)),
        out_shape=jax.ShapeDtypeStruct((M, ncols), out_dtype),
        scratch_shapes=[pltpu.VMEM((K, tn), BF16)],
        compiler_params=_cparams(2),
    )(x, w)


_NT = (((1,), (1,)), ((), ()))


def _mm_t_kernel(x_ref, w_ref, o_ref, wb_ref):
    @pl.when(pl.program_id(1) == 0)
    def _():
        wb_ref[...] = w_ref[...].astype(BF16)

    o_ref[...] = lax.dot_general(x_ref[...], wb_ref[...], _NT, preferred_element_type=F32).astype(o_ref.dtype)


def _mm_t_shift_kernel(x_ref, wa_ref, wt_ref, o_ref, wb_ref, *, shift):
    @pl.when(pl.program_id(1) == 0)
    def _():
        tn = wb_ref.shape[0]
        wb_ref[0:tn - shift, :] = wa_ref[shift:, :].astype(BF16)
        wb_ref[tn - shift:, :] = wt_ref[:shift, :].astype(BF16)

    o_ref[...] = lax.dot_general(x_ref[...], wb_ref[...], _NT, preferred_element_type=F32).astype(o_ref.dtype)


def _mm_t_copies_kernel(x_ref, w_ref, o_ref, *rest, n_rows, windows):
    copy_refs, wb_ref = rest[:-1], rest[-1]
    j, i = pl.program_id(0), pl.program_id(1)

    @pl.when(i == 0)
    def _():
        wb_ref[...] = w_ref[...].astype(BF16)

    acc = lax.dot_general(x_ref[...], wb_ref[...], _NT, preferred_element_type=F32)
    o_ref[...] = acc
    for c_ref, (j0, j1) in zip(copy_refs, windows):
        @pl.when(jnp.logical_and(i < n_rows, jnp.logical_and(j >= j0, j < j1)))
        def _(c_ref=c_ref):
            c_ref[...] = acc


def _matmul_t_with_copies(x, wt, l, nrows, copy_rows, copy_cols, *, tm=1024, tn=512):
    M, K = x.shape
    tm = _tile(math.gcd(M, copy_rows), tm)
    tn = _tile(nrows, tn)
    n_rows = copy_rows // tm
    windows = []
    for c0, c1 in copy_cols:
        assert c0 % tn == 0 and c1 % tn == 0
        windows.append((c0 // tn, c1 // tn))

    def copy_spec(j0, j1):
        def imap(j, i):
            inside = jnp.logical_and(j >= j0, j < j1)
            row = jnp.where(j < j0, 0, jnp.where(inside, jnp.minimum(i, n_rows - 1), n_rows - 1))
            col = jnp.where(j < j0, 0, jnp.where(inside, j - j0, j1 - j0 - 1))
            return (row, col)
        return pl.BlockSpec((tm, tn), imap)

    return pl.pallas_call(
        functools.partial(_mm_t_copies_kernel, n_rows=n_rows, windows=tuple(windows)),
        grid=(nrows // tn, M // tm),
        in_specs=[
            pl.BlockSpec((tm, K), lambda j, i: (i, 0)),
            pl.BlockSpec((None, tn, K), lambda j, i: (l, j, 0)),
        ],
        out_specs=[pl.BlockSpec((tm, tn), lambda j, i: (i, j))] + [copy_spec(j0, j1) for j0, j1 in windows],
        out_shape=[jax.ShapeDtypeStruct((M, nrows), F32)]
        + [jax.ShapeDtypeStruct((copy_rows, c1 - c0), F32) for c0, c1 in copy_cols],
        scratch_shapes=[pltpu.VMEM((tn, K), BF16)],
        compiler_params=_cparams(2),
    )(x, wt)


def _matmul_t(x, wt, l, row0, nrows, *, tm=1024, tn=512, out_dtype=F32):
    M, K = x.shape
    tm = _tile(M, tm)
    tn = _tile(nrows, tn)
    base = row0 // LANE * LANE
    shift = row0 - base
    assert base % tn == 0 and tn % LANE == 0 and shift % 32 == 0
    rb0 = base // tn
    in_specs = [
        pl.BlockSpec((tm, K), lambda j, i: (i, 0)),
        pl.BlockSpec((None, tn, K), lambda j, i: (l, rb0 + j, 0)),
    ]
    args = [x, wt]
    kern = _mm_t_kernel
    if shift:
        lpt = tn // LANE
        in_specs.append(pl.BlockSpec((None, LANE, K), lambda j, i: (l, (rb0 + j + 1) * lpt, 0)))
        args.append(wt)
        kern = functools.partial(_mm_t_shift_kernel, shift=shift)
    return pl.pallas_call(
        kern,
        grid=(nrows // tn, M // tm),
        in_specs=in_specs,
        out_specs=pl.BlockSpec((tm, tn), lambda j, i: (i, j)),
        out_shape=jax.ShapeDtypeStruct((M, nrows), out_dtype),
        scratch_shapes=[pltpu.VMEM((tn, K), BF16)],
        compiler_params=_cparams(2),
    )(*args)


def _mod_kernel(c_ref, w_ref, b_ref, o_ref):
    a = _silu(c_ref[...]).astype(BF16)
    o_ref[...] = jnp.dot(a, w_ref[...].astype(BF16), preferred_element_type=F32) + b_ref[...]


def _modulation(cond, w_mod, b_mod, *, tn=512):
    depth, D, N = w_mod.shape
    R = cond.shape[0]
    tn = _tile(N, tn)
    return pl.pallas_call(
        _mod_kernel,
        grid=(depth, N // tn),
        in_specs=[
            pl.BlockSpec((R, D), lambda l, j: (0, 0)),
            pl.BlockSpec((None, D, tn), lambda l, j: (l, 0, j)),
            pl.BlockSpec((None, 1, tn), lambda l, j: (l, 0, j)),
        ],
        out_specs=pl.BlockSpec((None, R, tn), lambda l, j: (l, 0, j)),
        out_shape=jax.ShapeDtypeStruct((depth, R, N), F32),
        compiler_params=_cparams(2),
    )(cond, w_mod, b_mod.reshape(depth, 1, N))


class _Slab:
    def __init__(self, t_ctx, dec_seq, n_dec):
        self.t_ctx, self.dec_seq, self.n_dec = t_ctx, dec_seq, n_dec
        self.T = t_ctx + dec_seq * n_dec

    def tile(self, pref):
        t = min(pref, self.t_ctx, self.dec_seq)
        while self.t_ctx % t or self.dec_seq % t:
            t //= 2
        return t

    def ctx_spec(self, tm, D):
        n_ctx = self.t_ctx // tm
        return pl.BlockSpec((tm, D), lambda i, *_: (jnp.minimum(i, n_ctx - 1), 0))

    def dec_spec(self, tm, D):
        n_ctx = self.t_ctx // tm
        return pl.BlockSpec((tm, D), lambda i, *_: (jnp.maximum(i - n_ctx, 0), 0))

    def mod_spec(self, tm, l, which, D):
        n_ctx = self.t_ctx // tm
        per = self.dec_seq // tm

        def imap(i, *_):
            return (l, jnp.where(i < n_ctx, 0, (i - n_ctx) // per + 1), which, 0, 0)

        return pl.BlockSpec((None, None, None, 1, D), imap)


def _premod_kernel(xc_ref, xd_ref, sc_ref, sh_ref, h_ref, *, n_ctx):
    x = _pick_rows(xc_ref, xd_ref, n_ctx)
    h_ref[...] = (x * (1.0 + sc_ref[...]) + sh_ref[...]).astype(h_ref.dtype)


def _pick_rows(c_ref, d_ref, n_ctx):
    return jnp.where(pl.program_id(0) < n_ctx, c_ref[...], d_ref[...])


def _premod(x_ctx, x_dec, mod5, slab, l):
    D = x_ctx.shape[1]
    tm = slab.tile(256)
    return pl.pallas_call(
        functools.partial(_premod_kernel, n_ctx=slab.t_ctx // tm),
        grid=(slab.T // tm,),
        in_specs=[slab.ctx_spec(tm, D), slab.dec_spec(tm, D), slab.mod_spec(tm, l, 1, D), slab.mod_spec(tm, l, 0, D)],
        out_specs=pl.BlockSpec((tm, D), lambda i: (i, 0)),
        out_shape=jax.ShapeDtypeStruct((slab.T, D), BF16),
        compiler_params=_cparams(1),
    )(x_ctx, x_dec, mod5, mod5)


def _exp_rows(s):
    e = jnp.exp(s - jnp.max(s, axis=-1, keepdims=True))
    return e, jnp.sum(e, axis=-1, keepdims=True)


def _diff_attn_rows(q, k1, k2, v, lam, scale, dk):
    nt = (((1,), (1,)), ((), ()))
    qs = q * scale
    e1, z1 = _exp_rows(lax.dot_general(qs[:, :dk].astype(BF16), k1, nt, preferred_element_type=F32))
    e2, z2 = _exp_rows(lax.dot_general(qs[:, dk:].astype(BF16), k2, nt, preferred_element_type=F32))
    o1 = jnp.dot(e1.astype(BF16), v, preferred_element_type=F32)
    o2 = jnp.dot(e2.astype(BF16), v, preferred_element_type=F32)
    return o1 * (1.0 / z1) - o2 * (lam / z2)


def _rms_rows(o, w):
    return o * lax.rsqrt(jnp.mean(o * o, axis=-1, keepdims=True) + LN_EPS) * w


def _attn_ctx_kernel(lam_ref, q_ref, k_ref, v_ref, nw_ref, o_ref, *, heads, dk, dv, post):
    lam = lam_ref[0, 0]
    scale = dk ** -0.5
    for h in range(heads):
        q = q_ref[:, h * 2 * dk:(h + 1) * 2 * dk]
        k = k_ref[:, h * 2 * dk:(h + 1) * 2 * dk].astype(BF16)
        v = v_ref[:, h * dv:(h + 1) * dv].astype(BF16)
        o = _diff_attn_rows(q, k[:, :dk], k[:, dk:], v, lam, scale, dk)
        o_ref[:, h * dv:(h + 1) * dv] = (_rms_rows(o, nw_ref[...]) * post).astype(o_ref.dtype)


def _attn_ctx(u, lam, norm_w, *, n_seq, L, heads, dk, dv, post):
    qw = heads * 2 * dk
    vw = heads * dv
    assert qw == vw
    kern = functools.partial(_attn_ctx_kernel, heads=heads, dk=dk, dv=dv, post=post)
    return pl.pallas_call(
        kern,
        grid=(n_seq,),
        in_specs=[
            pl.BlockSpec(memory_space=pltpu.SMEM),
            pl.BlockSpec((L, qw), lambda b: (b, 0)),
            pl.BlockSpec((L, qw), lambda b: (b, 1)),
            pl.BlockSpec((L, vw), lambda b: (b, 2)),
            pl.BlockSpec((1, dv), lambda b: (0, 0)),
        ],
        out_specs=pl.BlockSpec((L, vw), lambda b: (b, 0)),
        out_shape=jax.ShapeDtypeStruct((n_seq * L, vw), BF16),
        compiler_params=_cparams(1),
    )(lam, u, u, u, norm_w)


def _rope_rows(x, cos, s_lo, s_hi, dk):
    q4 = dk // 4
    return x * cos + pltpu.roll(x, dk - q4, 1) * s_lo + pltpu.roll(x, q4, 1) * s_hi


def _attn_dec_kernel(lam_ref, q_ref, k_ref, v_ref, ck_ref, cv_ref, cos_ref, slo_ref, shi_ref, nw_ref, o_ref,
                     kb_ref, vb_ref, *, dk, dv, past, L, tq, post):
    lam = lam_ref[0, 0]
    scale = dk ** -0.5
    cos, slo, shi = cos_ref[...], slo_ref[...], shi_ref[...]
    kb_ref[0:past, :] = ck_ref[...].astype(BF16)
    vb_ref[0:past, :] = cv_ref[...].astype(BF16)
    for c in range(2):
        kb_ref[past:past + L, c * dk:(c + 1) * dk] = _rope_rows(
            k_ref[:, c * dk:(c + 1) * dk], cos, slo, shi, dk).astype(BF16)
    vb_ref[past:past + L, :] = v_ref[...].astype(BF16)
    k1 = kb_ref[:, :dk]
    k2 = kb_ref[:, dk:]
    v = vb_ref[...]
    for i in range(L // tq):
        rows = slice(i * tq, (i + 1) * tq)
        q = jnp.concatenate(
            [_rope_rows(q_ref[rows, c * dk:(c + 1) * dk], cos[rows], slo[rows], shi[rows], dk) for c in range(2)],
            axis=1)
        o = _diff_attn_rows(q, k1, k2, v, lam, scale, dk)
        o_ref[rows, :] = (_rms_rows(o, nw_ref[...]) * post).astype(o_ref.dtype)


def _attn_dec(u, row0, cache_k, cache_v, l, rope, lam, norm_w, *, n_seq, L, heads, dk, dv, post):
    past = cache_k.shape[2]
    assert row0 % L == 0
    rb0 = row0 // L
    tq = _tile(L, 256)
    kern = functools.partial(_attn_dec_kernel, dk=dk, dv=dv, past=past, L=L, tq=tq, post=post)
    cos, slo, shi = rope
    tab = pl.BlockSpec((L, dk), lambda b, h: (0, 0))
    return pl.pallas_call(
        kern,
        grid=(n_seq, heads),
        in_specs=[
            pl.BlockSpec(memory_space=pltpu.SMEM),
            pl.BlockSpec((L, 2 * dk), lambda b, h: (rb0 + b, h)),
            pl.BlockSpec((L, 2 * dk), lambda b, h: (rb0 + b, heads + h)),
            pl.BlockSpec((L, dv), lambda b, h: (rb0 + b, 2 * heads + h)),
            pl.BlockSpec((None, None, past, 2 * dk), lambda b, h: (b, l, 0, h)),
            pl.BlockSpec((None, None, past, dv), lambda b, h: (b, l, 0, h)),
            tab, tab, tab,
            pl.BlockSpec((1, dv), lambda b, h: (0, 0)),
        ],
        out_specs=pl.BlockSpec((L, dv), lambda b, h: (b, h)),
        out_shape=jax.ShapeDtypeStruct((n_seq * L, heads * dv), BF16),
        scratch_shapes=[pltpu.VMEM((past + L, 2 * dk), BF16), pltpu.VMEM((past + L, dv), BF16)],
        compiler_params=_cparams(2),
    )(lam, u, u, u, cache_k, cache_v, cos, slo, shi, norm_w)


def _log_sigmoid(x):
    return -(jnp.maximum(-x, 0.0) + jnp.log1p(jnp.exp(-jnp.abs(x))))


def _split3_bf16(x):
    hi = x.astype(BF16)
    r1 = x - hi.astype(F32)
    mid = r1.astype(BF16)
    lo = (r1 - mid.astype(F32)).astype(BF16)
    return hi, mid, lo


def _gla_kernel(*refs, L, K, V, G, has_s0, want_state):
    q_ref, k_ref, v_ref, r_ref, ag_ref, a2_ref, ab_ref, nw_ref = refs[:8]
    refs = refs[8:]
    s0_ref = None
    if has_s0:
        s0_ref, refs = refs[0], refs[1:]
    o_ref, refs = refs[0], refs[1:]
    sfin_ref = None
    if want_state:
        sfin_ref, refs = refs[0], refs[1:]
    st_ref, of_ref, ob_ref = refs

    C = GLA_CHUNK
    N = L // C
    r2 = a2_ref.shape[1]
    row = lax.broadcasted_iota(jnp.int32, (C, C), 0)
    col = lax.broadcasted_iota(jnp.int32, (C, C), 1)
    nt = (((1,), (1,)), ((), ()))
    tn = (((0,), (0,)), ((), ()))
    keeps = (col <= row, col >= row)
    tris = tuple(jnp.where(kp, 1.0, 0.0).astype(BF16) for kp in keeps)

    for d in range(2):
        for g in range(G):
            st_ref[d, g] = s0_ref[d, g].T if has_s0 else jnp.zeros((V, K), F32)

    o_refs = (of_ref, ob_ref)

    def body(i, carry):
        sls, q_in, k_in, k_out, dec = [], [], [], [], []
        for d in range(2):
            n = i if d == 0 else N - 1 - i
            sl = pl.ds(pl.multiple_of(n * C, C), C)
            agc = ag_ref[sl, :][:, :r2].astype(BF16)
            logit = jnp.dot(agc, a2_ref[d].astype(BF16), preferred_element_type=F32) + ab_ref[d]
            la = _log_sigmoid(logit) * (1.0 / GLA_TAU)
            hi, mid, lo = _split3_bf16(la)
            bc = (jnp.dot(tris[d], hi, preferred_element_type=F32) + jnp.dot(tris[d], mid, preferred_element_type=F32)
                  + jnp.dot(tris[d], lo, preferred_element_type=F32))
            blast = bc[C - 1:C, :] if d == 0 else bc[0:1, :]
            kc = k_ref[sl, :]
            sls.append(sl)
            q_in.append((q_ref[sl, :] * (K ** -0.5) * jnp.exp(bc)).astype(BF16))
            k_in.append((kc * jnp.exp(-bc)).astype(BF16))
            k_out.append((kc * jnp.exp(blast - bc)).astype(BF16))
            dec.append(jnp.exp(blast))
        pairs = [(d, g) for d in range(2) for g in range(G)]
        ksl = lambda g: slice(g * K, (g + 1) * K)
        vsl = lambda g: slice(g * V, (g + 1) * V)
        att = {(d, g): lax.dot_general(q_in[d][:, ksl(g)], k_in[d][:, ksl(g)], nt, preferred_element_type=F32)
               for d, g in pairs}
        for d, g in pairs:
            o_refs[d][sls[d], vsl(g)] = lax.dot_general(q_in[d][:, ksl(g)], st_ref[d, g].astype(BF16), nt,
                                                        preferred_element_type=F32)
        for d, g in pairs:
            vc = v_ref[sls[d], vsl(g)].astype(BF16)
            ds_t = lax.dot_general(vc, k_out[d][:, ksl(g)], tn, preferred_element_type=F32)
            st_ref[d, g] = dec[d][:, ksl(g)] * st_ref[d, g] + ds_t
        for d, g in pairs:
            vc = v_ref[sls[d], vsl(g)].astype(BF16)
            a = jnp.where(keeps[d], att[d, g], 0.0).astype(BF16)
            o_refs[d][sls[d], vsl(g)] += jnp.dot(a, vc, preferred_element_type=F32)
        return carry

    lax.fori_loop(0, N, body, 0)
    if want_state:
        for d in range(2):
            for g in range(G):
                sfin_ref[d, g] = st_ref[d, g].T
    for g in range(G):
        vs = slice(g * V, (g + 1) * V)
        og = _rms_rows(of_ref[:, vs] + ob_ref[:, vs], nw_ref[...]) * _silu(r_ref[:, vs])
        o_ref[:, vs] = og.astype(o_ref.dtype)


def _gla(u, ag, row0, a2p, ab, norm_w, s0, *, n_seq, L, heads, K, V, qcol, vcol, rcol, want_state):
    G = _tile(heads, 8 if L <= 256 else 4)
    gk, gv = G * K, G * V
    assert row0 % L == 0 and qcol % gk == 0 and (heads * K) % gk == 0 and vcol % gv == 0 and rcol % gv == 0
    rb0 = row0 // L
    qb, kb, vb, rb = qcol // gk, (qcol + heads * K) // gk, vcol // gv, rcol // gv
    r2 = a2p.shape[1]
    has_s0 = s0 is not None
    kern = functools.partial(_gla_kernel, L=L, K=K, V=V, G=G, has_s0=has_s0, want_state=want_state)
    in_specs = [
        pl.BlockSpec((L, gk), lambda b, h: (rb0 + b, qb + h)),
        pl.BlockSpec((L, gk), lambda b, h: (rb0 + b, kb + h)),
        pl.BlockSpec((L, gv), lambda b, h: (rb0 + b, vb + h)),
        pl.BlockSpec((L, gv), lambda b, h: (rb0 + b, rb + h)),
        pl.BlockSpec((L, LANE), lambda b, h: (rb0 + b, 0)),
        pl.BlockSpec((2, r2, gk), lambda b, h: (0, 0, h)),
        pl.BlockSpec((2, 1, gk), lambda b, h: (0, 0, h)),
        pl.BlockSpec((1, V), lambda b, h: (0, 0)),
    ]
    args = [u, u, u, u, ag, a2p, ab, norm_w]
    if has_s0:
        in_specs.append(pl.BlockSpec((None, 2, G, K, V), lambda b, h: (b, 0, h, 0, 0)))
        args.append(s0)
    out_specs = [pl.BlockSpec((L, gv), lambda b, h: (b, h))]
    out_shape = [jax.ShapeDtypeStruct((n_seq * L, heads * V), BF16)]
    if want_state:
        out_specs.append(pl.BlockSpec((None, 2, G, K, V), lambda b, h: (b, 0, h, 0, 0)))
        out_shape.append(jax.ShapeDtypeStruct((n_seq, 2, heads, K, V), F32))
    res = pl.pallas_call(
        kern,
        grid=(n_seq, heads // G),
        in_specs=in_specs,
        out_specs=out_specs,
        out_shape=out_shape,
        scratch_shapes=[pltpu.VMEM((2, G, V, K), F32), pltpu.VMEM((L, gv), F32), pltpu.VMEM((L, gv), F32)],
        compiler_params=_cparams(2),
    )(*args)
    return res if want_state else (res[0], None)


def _merge_kernel(odc_ref, odd_ref, ogc_ref, ogd_ref, wd_ref, wg_ref, gd_ref, gg_ref, bd_ref, bg_ref, o_ref,
                  wdb_ref, wgb_ref, *, n_ctx):
    i = pl.program_id(1)

    @pl.when(i == 0)
    def _():
        wdb_ref[...] = wd_ref[...].astype(BF16)
        wgb_ref[...] = wg_ref[...].astype(BF16)

    def run(od_ref, og_ref):
        yd = jnp.dot(od_ref[...], wdb_ref[...], preferred_element_type=F32)
        yg = jnp.dot(og_ref[...], wgb_ref[...], preferred_element_type=F32)
        m = _sigmoid(gd_ref[...] + bd_ref[...]) * yd + _sigmoid(gg_ref[...] + bg_ref[...]) * yg
        o_ref[...] = m.astype(o_ref.dtype)

    @pl.when(i < n_ctx)
    def _():
        run(odc_ref, ogc_ref)

    @pl.when(i >= n_ctx)
    def _():
        run(odd_ref, ogd_ref)


def _merge(od_c, od_d, og_c, og_d, w_da, w_gla, gates, b_gate4, l, *, tm=512, tn=512):
    t_ctx, Kd = od_c.shape
    t_dec = od_d.shape[0]
    Kg = og_c.shape[1]
    D = w_da.shape[2]
    tm = _tile(math.gcd(t_ctx, t_dec), tm)
    tn = _tile(D, tn)
    nj = D // tn
    n_ctx = t_ctx // tm
    T = t_ctx + t_dec
    ctx_rows = lambda j, i: (jnp.minimum(i, n_ctx - 1), 0)
    dec_rows = lambda j, i: (jnp.maximum(i - n_ctx, 0), 0)
    return pl.pallas_call(
        functools.partial(_merge_kernel, n_ctx=n_ctx),
        grid=(nj, T // tm),
        in_specs=[
            pl.BlockSpec((tm, Kd), ctx_rows),
            pl.BlockSpec((tm, Kd), dec_rows),
            pl.BlockSpec((tm, Kg), ctx_rows),
            pl.BlockSpec((tm, Kg), dec_rows),
            pl.BlockSpec((None, Kd, tn), lambda j, i: (l, 0, j)),
            pl.BlockSpec((None, Kg, tn), lambda j, i: (l, 0, j)),
            pl.BlockSpec((tm, tn), lambda j, i: (i, j)),
            pl.BlockSpec((tm, tn), lambda j, i: (i, nj + j)),
            pl.BlockSpec((None, None, 1, tn), lambda j, i: (l, 0, 0, j)),
            pl.BlockSpec((None, None, 1, tn), lambda j, i: (l, 1, 0, j)),
        ],
        out_specs=pl.BlockSpec((tm, tn), lambda j, i: (i, j)),
        out_shape=jax.ShapeDtypeStruct((T, D), BF16),
        scratch_shapes=[pltpu.VMEM((Kd, tn), BF16), pltpu.VMEM((Kg, tn), BF16)],
        compiler_params=_cparams(2),
    )(od_c, od_d, og_c, og_d, w_da, w_gla, gates, gates, b_gate4, b_gate4)


def _layer_norm_rows(z, g, b):
    mu = jnp.mean(z, axis=-1, keepdims=True)
    zc = z - mu
    var = jnp.mean(zc * zc, axis=-1, keepdims=True)
    return zc * lax.rsqrt(var + LN_EPS) * g + b


def _post_mixer_kernel(xc_ref, xd_ref, y_ref, g1_ref, sc_ref, sh_ref, lg_ref, lb_ref, wr_ref, br_ref,
                       x1_ref, h2_ref, lo_ref, *, alpha, n_ctx):
    x = _pick_rows(xc_ref, xd_ref, n_ctx)
    x1 = _layer_norm_rows(alpha * x + g1_ref[...] * y_ref[...], lg_ref[...], lb_ref[...])
    x1_ref[...] = x1
    h2 = x1 * (1.0 + sc_ref[...]) + sh_ref[...]
    h2_ref[...] = h2.reshape(h2_ref.shape)
    lo_ref[...] = jnp.dot(h2, wr_ref[...], preferred_element_type=F32,
                          precision=lax.Precision.HIGHEST) + br_ref[...]


def _post_mixer(x_ctx, x_dec, y, mod5, slab, l, ln_g, ln_b, wr, br, *, alpha):
    T, D = y.shape
    tm = slab.tile(256)
    nr = wr.shape[2]
    row = pl.BlockSpec((tm, D), lambda i: (i, 0))
    vec = pl.BlockSpec((None, 1, D), lambda i: (l, 0, 0))
    return pl.pallas_call(
        functools.partial(_post_mixer_kernel, alpha=alpha, n_ctx=slab.t_ctx // tm),
        grid=(T // tm,),
        in_specs=[slab.ctx_spec(tm, D), slab.dec_spec(tm, D), row,
                  slab.mod_spec(tm, l, 2, D), slab.mod_spec(tm, l, 4, D), slab.mod_spec(tm, l, 3, D),
                  vec, vec,
                  pl.BlockSpec((None, D, nr), lambda i: (l, 0, 0)),
                  pl.BlockSpec((None, 1, nr), lambda i: (l, 0, 0))],
        out_specs=[row, pl.BlockSpec((tm, 1, D), lambda i: (i, 0, 0)), pl.BlockSpec((tm, nr), lambda i: (i, 0))],
        out_shape=[jax.ShapeDtypeStruct((T, D), F32), jax.ShapeDtypeStruct((T, 1, D), F32),
                   jax.ShapeDtypeStruct((T, nr), F32)],
        compiler_params=_cparams(1),
    )(x_ctx, x_dec, y, mod5, mod5, mod5, ln_g, ln_b, wr, br)


def _issue_rows(copy_fn, tm, priority=None):
    def f(i, c):
        for q in range(ROW_SEMS):
            for cp in copy_fn(i * ROW_SEMS + q, q):
                cp.start(priority=q % 2 if priority is None else priority)
        return c
    lax.fori_loop(0, tm // ROW_SEMS, f, 0, unroll=DMA_UNROLL // ROW_SEMS)


def _wait_rows(src_hbm, dst, sems):
    n = dst.shape[0] // ROW_SEMS
    for q in range(ROW_SEMS):
        pltpu.make_async_copy(src_hbm.at[pl.ds(0, n)], dst.at[pl.ds(0, n)], sems.at[q]).wait()


def _expert_changed(blk_e_ref, b):
    return jnp.logical_or(b == 0, blk_e_ref[b] != blk_e_ref[jnp.maximum(b - 1, 0)])


def _moe_ffn_kernel(blk_e_ref, nxt_e_ref, nused_ref, tok_ref, x_hbm, w1_hbm, w3_hbm, w2_hbm, o_ref,
                    rows_ref, dense_ref, s1_ref, s3_ref, s2_ref, b1_ref, b3_ref, b2_ref, sem, row_sem, *, l, tm):
    b = pl.program_id(0)
    nused = nused_ref[0]
    used = b < nused
    streams = ((w1_hbm, s1_ref, b1_ref), (w3_hbm, s3_ref, b3_ref), (w2_hbm, s2_ref, b2_ref))

    def issue_rows(blk, slot):
        def copy_fn(r, q):
            t = tok_ref[blk * tm + r]
            return (pltpu.make_async_copy(x_hbm.at[pl.ds(t, 1)], rows_ref.at[slot, pl.ds(r, 1)],
                                          row_sem.at[slot, q]),)
        _issue_rows(copy_fn, tm, priority=1)

    @pl.when(jnp.logical_and(b == 0, nused > 0))
    def _():
        issue_rows(0, 0)

    @pl.when(b + 1 < nused)
    def _():
        issue_rows(b + 1, (b + 1) % 2)

    def fetch(k, e):
        w_hbm, stage_ref, _ = streams[k]
        return pltpu.make_async_copy(w_hbm.at[l, e], stage_ref, sem.at[k])

    @pl.when(jnp.logical_and(used, _expert_changed(blk_e_ref, b)))
    def _():
        e = blk_e_ref[b]
        nxt = nxt_e_ref[b]

        @pl.when(b == 0)
        def _():
            for k in range(3):
                fetch(k, e).start()

        for k in range(3):
            _, stage_ref, cast_ref = streams[k]
            fetch(k, e).wait()
            cast_ref[...] = stage_ref[...].astype(BF16)

            @pl.when(nxt >= 0)
            def _(k=k):
                fetch(k, nxt).start()

    @pl.when(used)
    def _():
        _wait_rows(x_hbm, rows_ref.at[b % 2], row_sem.at[b % 2])
        dense_ref[...] = rows_ref[b % 2].reshape(dense_ref.shape)
        xs = dense_ref[...].astype(BF16)
        h1 = jnp.dot(xs, b1_ref[...], preferred_element_type=F32)
        h3 = jnp.dot(xs, b3_ref[...], preferred_element_type=F32)
        hid = (_silu(h1) * h3).astype(BF16)
        o_ref[...] = jnp.dot(hid, b2_ref[...], preferred_element_type=F32)

    @pl.when(jnp.logical_not(used))
    def _():
        o_ref[...] = jnp.zeros_like(o_ref)


def _moe(h2, blk_e, nxt_e, nused, buf_tok, w1, w3, w2, l, *, tm):
    D, FF = w1.shape[2:]
    P = buf_tok.shape[0]
    grid_spec = pltpu.PrefetchScalarGridSpec(
        num_scalar_prefetch=4,
        grid=(P // tm,),
        in_specs=[pl.BlockSpec(memory_space=pl.ANY)] * 4,
        out_specs=pl.BlockSpec((tm, D), lambda b, *_: (b, 0)),
        scratch_shapes=[pltpu.VMEM((2, tm, 1, D), F32), pltpu.VMEM((tm, D), F32),
                        pltpu.VMEM((D, FF), F32), pltpu.VMEM((D, FF), F32), pltpu.VMEM((FF, D), F32),
                        pltpu.VMEM((D, FF), BF16), pltpu.VMEM((D, FF), BF16), pltpu.VMEM((FF, D), BF16),
                        pltpu.SemaphoreType.DMA((3,)), pltpu.SemaphoreType.DMA((2, ROW_SEMS))],
    )
    return pl.pallas_call(
        functools.partial(_moe_ffn_kernel, l=l, tm=tm),
        grid_spec=grid_spec,
        out_shape=jax.ShapeDtypeStruct((P, D), F32),
        compiler_params=_cparams(1),
    )(blk_e, nxt_e, nused, buf_tok, h2, w1, w3, w2)


def _post_moe_kernel(p0_ref, p1_ref, es_hbm, x_ref, gw_ref, g2_ref, lg_ref, lb_ref, *rest, tm, alpha, has_next,
                     n_ctx):
    if has_next:
        sc_ref, sh_ref, x2c_ref, x2d_ref, h_ref, buf_ref, sem = rest
    else:
        x2c_ref, x2d_ref, buf_ref, sem = rest
    i = pl.program_id(0)
    n = pl.num_programs(0)
    p_refs = (p0_ref, p1_ref)

    def issue(tile, slot):
        def copy_fn(r, q):
            return tuple(pltpu.make_async_copy(es_hbm.at[pl.ds(p_refs[j][tile * tm + r], 1)],
                                               buf_ref.at[slot, j, pl.ds(r, 1)], sem.at[slot, j, q])
                         for j in range(TOP_K))
        _issue_rows(copy_fn, tm)

    @pl.when(i == 0)
    def _():
        issue(0, 0)

    @pl.when(i + 1 < n)
    def _():
        issue(i + 1, (i + 1) % 2)

    slot = i % 2
    for j in range(TOP_K):
        _wait_rows(es_hbm, buf_ref.at[slot, j], sem.at[slot, j])
    y = gw_ref[:, 0:1] * buf_ref[slot, 0] + gw_ref[:, 1:2] * buf_ref[slot, 1]
    x2 = _layer_norm_rows(alpha * x_ref[...] + g2_ref[...] * y, lg_ref[...], lb_ref[...])

    @pl.when(i < n_ctx)
    def _():
        x2c_ref[...] = x2

    @pl.when(i >= n_ctx)
    def _():
        x2d_ref[...] = x2

    if has_next:
        h_ref[...] = (x2 * (1.0 + sc_ref[...]) + sh_ref[...]).astype(h_ref.dtype)


def _post_moe(p0, p1, es, x1, gate, mod5, slab, l, ln_g, ln_b, *, alpha, has_next):
    T, D = x1.shape
    tm = slab.tile(256)
    row = pl.BlockSpec((tm, D), lambda i, *_: (i, 0))
    vec = pl.BlockSpec((None, 1, D), lambda i, *_: (l, 0, 0))
    in_specs = [pl.BlockSpec(memory_space=pl.ANY), row, pl.BlockSpec((tm, TOP_K), lambda i, *_: (i, 0)),
                slab.mod_spec(tm, l, 5, D), vec, vec]
    args = [es, x1, gate, mod5, ln_g, ln_b]
    out_specs = [slab.ctx_spec(tm, D), slab.dec_spec(tm, D)]
    out_shape = [jax.ShapeDtypeStruct((slab.t_ctx, D), F32), jax.ShapeDtypeStruct((T - slab.t_ctx, D), F32)]
    if has_next:
        in_specs += [slab.mod_spec(tm, l + 1, 1, D), slab.mod_spec(tm, l + 1, 0, D)]
        args += [mod5, mod5]
        out_specs.append(row)
        out_shape.append(jax.ShapeDtypeStruct((T, D), BF16))
    grid_spec = pltpu.PrefetchScalarGridSpec(
        num_scalar_prefetch=2,
        grid=(T // tm,),
        in_specs=in_specs,
        out_specs=out_specs,
        scratch_shapes=[pltpu.VMEM((2, TOP_K, tm, D), F32), pltpu.SemaphoreType.DMA((2, TOP_K, ROW_SEMS))],
    )
    res = pl.pallas_call(
        functools.partial(_post_moe_kernel, tm=tm, alpha=alpha, has_next=has_next, n_ctx=slab.t_ctx // tm),
        grid_spec=grid_spec,
        out_shape=out_shape,
        compiler_params=_cparams(1),
    )(p0, p1, *args)
    return (res[0], res[1], res[2] if has_next else None)


def _route(logits, n_groups, epg, tm):
    T = logits.shape[0]
    n_exp = n_groups * epg
    pg = jax.nn.softmax(logits[:, :n_groups], axis=-1)
    g_idx = jnp.argmax(pg, axis=-1)
    p_sel = jnp.take_along_axis(pg, g_idx[:, None], axis=-1)
    le = logits[:, n_groups:n_groups + n_exp].reshape(T, n_groups, epg)
    le = jnp.take_along_axis(le, g_idx[:, None, None], axis=1)[:, 0]
    pe = jax.nn.softmax(le, axis=-1)
    top_p, top_i = lax.top_k(pe, TOP_K)
    gate = p_sel * top_p / jnp.sum(top_p, axis=-1, keepdims=True)
    expert = (g_idx[:, None] * epg + top_i).astype(jnp.int32)

    A = T * TOP_K
    e_flat = expert.reshape(A)
    tok_flat = jnp.repeat(jnp.arange(T, dtype=jnp.int32), TOP_K)
    onehot = (e_flat[:, None] == jnp.arange(n_exp, dtype=jnp.int32)[None, :]).astype(jnp.int32)
    csum = jnp.cumsum(onehot, axis=0)
    counts = csum[-1]
    rank = jnp.take_along_axis(csum, e_flat[:, None], axis=1)[:, 0] - 1
    padded = (counts + tm - 1) // tm * tm
    pends = jnp.cumsum(padded)
    pstarts = pends - padded
    pos = (pstarts[e_flat] + rank).astype(jnp.int32)
    n_blocks = -(-A // tm) + n_exp
    P = n_blocks * tm
    buf_tok = jnp.zeros((P,), jnp.int32).at[pos].set(tok_flat)
    blk_e = jnp.clip(jnp.searchsorted(pends, jnp.arange(n_blocks, dtype=pends.dtype) * tm, side='right'),
                     0, n_exp - 1).astype(jnp.int32)
    nused = (pends[-1] // tm).astype(jnp.int32).reshape(1)
    ids = jnp.arange(n_exp, dtype=jnp.int32)
    own = jnp.where(counts > 0, ids, n_exp)
    later = lax.cummin(jnp.concatenate([own[1:], jnp.full((1,), n_exp, jnp.int32)]), reverse=True)
    nxt_e = jnp.where(later < n_exp, later, -1)[blk_e].astype(jnp.int32)
    pos2 = pos.reshape(T, TOP_K)
    return blk_e, nxt_e, nused, buf_tok, gate.astype(F32), pos2[:, 0], pos2[:, 1]


def _rope_tables(n_tokens, dk):
    freqs = dk // 4
    rows = n_tokens // GRID_W
    row = jnp.repeat(jnp.arange(rows, dtype=F32), GRID_W)
    col = jnp.tile(jnp.arange(GRID_W, dtype=F32), rows)
    inv = 1.0 / (ROPE_THETA ** (jnp.arange(freqs, dtype=F32) / freqs))
    ar, ac = row[:, None] * inv, col[:, None] * inv
    z = jnp.zeros_like(ar)
    cos = jnp.concatenate([jnp.cos(ar), jnp.cos(ar), jnp.cos(ac), jnp.cos(ac)], axis=1)
    s_lo = jnp.concatenate([-jnp.sin(ar), z, -jnp.sin(ac), z], axis=1)
    s_hi = jnp.concatenate([z, jnp.sin(ar), z, jnp.sin(ac)], axis=1)
    return cos, s_lo, s_hi


def kernel(x_prompt, x_sample, cache_k, cache_v, state_gla, c, c_ctx, w_mod, b_mod, w_in, b_gate, da_lambda,
           da_norm_w, gla_a2, gla_a_bias, gla_norm_w, w_da_proj, w_gla_proj, w_out, ln1_g, ln1_b, ln2_g, ln2_b,
           router_g_w, router_g_b, router_e_w, router_e_b, exp_w1, exp_w3, exp_w2):
    B, S, D = x_prompt.shape
    n_dec, dec_seq, _ = x_sample.shape
    depth = w_mod.shape[0]
    past, da_heads, _, dk = cache_k.shape[2:]
    dv = cache_v.shape[-1]
    gla_heads, gk, gv = state_gla.shape[3:]
    lowrank = gla_a2.shape[2]
    n_groups, epg = router_e_w.shape[2:]
    n_exp = n_groups * epg
    da_qw, da_vw = da_heads * 2 * dk, da_heads * dv
    gla_qw, gla_vw = gla_heads * gk, gla_heads * gv
    n_main = 2 * da_qw + da_vw + 2 * gla_qw + 2 * gla_vw
    assert n_main % LANE == 0 and w_in.shape[2] == n_main + 2 * lowrank + 2 * D
    alpha = (2.0 * depth) ** 0.25

    slab = _Slab(B * S, dec_seq, n_dec)
    t_ctx = slab.t_ctx
    moe_tm = 128

    x_ctx, x_dec = x_prompt.reshape(t_ctx, D), x_sample.reshape(n_dec * dec_seq, D)
    n_cond = 8
    cond = jnp.zeros((n_cond, D), F32).at[0].set(c_ctx).at[1:1 + n_dec].set(c)
    w_in_t = jnp.swapaxes(w_in, 1, 2)
    b_gate4 = b_gate.reshape(depth, 2, 1, D)
    nr =-(-(n_groups + n_exp) // LANE) * LANE
    wr = jnp.concatenate([router_g_w, router_e_w.reshape(depth, D, n_exp),
                          jnp.zeros((depth, D, nr - n_groups - n_exp), F32)], axis=2)
    br = jnp.concatenate([router_g_b, router_e_b.reshape(depth, n_exp),
                          jnp.zeros((depth, nr - n_groups - n_exp), F32)], axis=1).reshape(depth, 1, nr)
    a2p = jnp.zeros((depth, 2, 2 * lowrank, gla_qw), F32)
    a2p = a2p.at[:, 0, :lowrank].set(gla_a2[:, 0]).at[:, 1, lowrank:].set(gla_a2[:, 1])
    ab = gla_a_bias.reshape(depth, 2, 1, gla_qw)
    lv = da_lambda.astype(F32)
    lam_init = [0.8 - 0.6 * math.exp(-0.3 * l) for l in range(depth)]
    lam = [(jnp.exp(jnp.sum(lv[l, 0] * lv[l, 1])) - jnp.exp(jnp.sum(lv[l, 2] * lv[l, 3])) + lam_init[l]).reshape(1, 1)
           for l in range(depth)]
    rope = _rope_tables(dec_seq, dk)
    cache_k2 = cache_k.reshape(n_dec, depth, past, da_qw)
    cache_v2 = cache_v.reshape(n_dec, depth, past, da_vw)
    ln1_g3, ln1_b3 = ln1_g.reshape(depth, 1, D), ln1_b.reshape(depth, 1, D)
    ln2_g3, ln2_b3 = ln2_g.reshape(depth, 1, D), ln2_b.reshape(depth, 1, D)

    mod = _modulation(cond, w_mod, b_mod)
    mod5 = mod.reshape(depth, n_cond, 6, 1, D)

    h = _premod(x_ctx, x_dec, mod5, slab, 0)
    ks, vs, sts = [], [], []
    for l in range(depth):
        u, k_ctx, v_ctx = _matmul_t_with_copies(
            h, w_in_t, l, n_main, t_ctx, [(da_qw, 2 * da_qw), (2 * da_qw, 2 * da_qw + da_vw)])
        ag = _matmul_t(h, w_in_t, l, n_main, LANE, tn=LANE)
        gates = _matmul_t(h, w_in_t, l, n_main + 2 * lowrank, 2 * D)
        ks.append(k_ctx.reshape(B, S, da_qw))
        vs.append(v_ctx.reshape(B, S, da_vw))

        post = 1.0 - lam_init[l]
        nw_d = da_norm_w[l].reshape(1, dv)
        od_c = _attn_ctx(u, lam[l], nw_d, n_seq=B, L=S, heads=da_heads, dk=dk, dv=dv, post=post)
        od_d = _attn_dec(u, t_ctx, cache_k2, cache_v2, l, rope, lam[l], nw_d,
                         n_seq=n_dec, L=dec_seq, heads=da_heads, dk=dk, dv=dv, post=post)

        qcol = 2 * da_qw + da_vw
        vcol = qcol + 2 * gla_qw
        rcol = vcol + gla_vw
        nw_g = gla_norm_w[l].reshape(1, gv)
        gla_kw = dict(heads=gla_heads, K=gk, V=gv, qcol=qcol, vcol=vcol, rcol=rcol)
        og_c, st = _gla(u, ag, 0, a2p[l], ab[l], nw_g, None, n_seq=B, L=S, want_state=True, **gla_kw)
        og_d, _ = _gla(u, ag, t_ctx, a2p[l], ab[l], nw_g, state_gla[:, l], n_seq=n_dec, L=dec_seq,
                       want_state=False, **gla_kw)
        sts.append(st)

        m = _merge(od_c, od_d, og_c, og_d, w_da_proj, w_gla_proj, gates, b_gate4, l)
        y = _matmul(m, w_out, l, 0, D)
        x1, h2, logits = _post_mixer(x_ctx, x_dec, y, mod5, slab, l, ln1_g3, ln1_b3, wr, br, alpha=alpha)

        blk_e, nxt_e, nused, buf_tok, gate, p0, p1 = _route(logits, n_groups, epg, moe_tm)
        es = _moe(h2, blk_e, nxt_e, nused, buf_tok, exp_w1, exp_w3, exp_w2, l, tm=moe_tm)
        x_ctx, x_dec, h = _post_moe(p0, p1, es, x1, gate, mod5, slab, l, ln2_g3, ln2_b3, alpha=alpha,
                                    has_next=l + 1 < depth)

    y_prompt = x_ctx.reshape(B, S, D)
    y_sample = x_dec.reshape(n_dec, dec_seq, D)
    new_k = jnp.stack(ks, axis=1).reshape(B, depth, S, da_heads, 2, dk)
    new_v = jnp.stack(vs, axis=1).reshape(B, depth, S, da_heads, dv)
    return (y_prompt, y_sample, new_k, new_v, jnp.stack(sts, axis=1))
```
